```python
import jax, jax.numpy as jnp
from jax import lax
import numpy as np

D_MODEL = 2048
BATCH = 4
SEQ = 2048
DEPTH = 4

CHUNK = 64
Q_BLOCK = 128
GLA_HEADS = 4
GLA_DK = D_MODEL // 2 // GLA_HEADS
GLA_DV = D_MODEL // GLA_HEADS
GLA_K = GLA_HEADS * GLA_DK
GLA_V = GLA_HEADS * GLA_DV
GLA_GATE_RANK = 16
GLA_GATE_TAU = 16.0
SB_HEADS = 16
SB_DH = D_MODEL // SB_HEADS
SB_W = SB_HEADS * SB_DH
D_FF = 4 * D_MODEL
EPS = 1e-6
IN_SPLITS = (GLA_K, GLA_K, GLA_V, GLA_V, GLA_GATE_RANK, SB_W, SB_W, SB_W, D_MODEL, D_MODEL)
IN_COLS = GLA_K * 2 + GLA_V * 2 + GLA_GATE_RANK + SB_W * 3 + D_MODEL * 2

kernel_name = "hybrid_gla_stickbreaking_sandwich_adaln"


def rmsnorm(x, gain):
    xf = x.astype(jnp.float32)
    y = xf * lax.rsqrt(jnp.mean(xf * xf, axis=-1, keepdims=True) + EPS)
    return (y * gain.astype(jnp.float32)).astype(x.dtype)


def gla_branch(q, k, v, r, a_low, w_gate_up, b_gate, gn_gain):
    B, S, _ = q.shape
    nc = S // CHUNK
    log_a = jax.nn.log_sigmoid((a_low @ w_gate_up + b_gate).astype(jnp.float32)) / GLA_GATE_TAU

    def heads(t, d):
        return t.astype(jnp.float32).reshape(B, nc, CHUNK, GLA_HEADS, d).transpose(1, 0, 3, 2, 4)

    qh = heads(q, GLA_DK) * (GLA_DK ** -0.5)
    kh = heads(k, GLA_DK)
    vh = heads(v, GLA_DV)
    gh = heads(log_a, GLA_DK)

    def step(state, inp):
        qc, kc, vc, gc = inp
        g = jnp.cumsum(gc, axis=2)
        g_tot = g[:, :, -1:, :]
        kv = jnp.einsum('bhck,bhcv->bhkv', kc * jnp.exp(g_tot - g), vc)
        state = jnp.exp(g_tot[:, :, 0, :, None]) * state + kv
        out = jnp.einsum('bhck,bhkv->bhcv', qc, state)
        return state, out

    s0 = jnp.zeros((B, GLA_HEADS, GLA_DK, GLA_DV), jnp.float32)
    _, o = lax.scan(step, s0, (qh, kh, vh, gh))
    o = o.transpose(1, 0, 3, 2, 4).reshape(B, S, GLA_HEADS, GLA_DV).astype(q.dtype)
    o = rmsnorm(o, gn_gain).reshape(B, S, GLA_V)
    return o * jax.nn.silu(r)


def sb_branch(q, k, v):
    B, S, _ = q.shape

    def heads(t):
        return t.reshape(B, S, SB_HEADS, SB_DH).transpose(0, 2, 1, 3)

    qh, kh, vh = heads(q), heads(k), heads(v)
    scale = SB_DH ** -0.5
    outs = []
    for blk in range(S // Q_BLOCK):
        q0 = blk * Q_BLOCK
        end = q0 + Q_BLOCK
        qb = qh[:, :, q0:end]
        kb = kh[:, :, :end]
        vb = vh[:, :, :end]
        z = jnp.einsum('bhqd,bhkd->bhqk', qb, kb).astype(jnp.float32) * scale
        t_idx = q0 + jnp.arange(Q_BLOCK)[:, None]
        s_idx = jnp.arange(end)[None, :]
        past = s_idx < t_idx
        log_beta = jax.nn.log_sigmoid(z)
        log_1mb = jnp.where(past, jax.nn.log_sigmoid(-z), 0.0)
        after = lax.cumsum(log_1mb, axis=3, reverse=True) - log_1mb
        w = jnp.where(past, jnp.exp(log_beta + after), 0.0)
        outs.append(jnp.einsum('bhqk,bhkd->bhqd', w.astype(vb.dtype), vb))
    o = jnp.concatenate(outs, axis=2)
    return o.transpose(0, 2, 1, 3).reshape(B, S, SB_W)


def setup_inputs(seed: int = 0) -> dict:
    key = jax.random.key(seed)
    ks = jax.random.split(key, 16)
    L, D = DEPTH, D_MODEL

    def nrm(k, shape, fan_in):
        return jax.random.normal(k, shape, jnp.float32) * (fan_in ** -0.5)

    def gain(k, shape):
        return 1.0 + 0.05 * jax.random.normal(k, shape, jnp.float32)

    return {
        "x": jax.random.normal(ks[0], (BATCH, SEQ, D), jnp.float32),
        "c": jax.random.normal(ks[1], (BATCH, D), jnp.float32),
        "w_ada": nrm(ks[2], (L, D, 6 * D), D),
        "b_ada": 0.02 * jax.random.normal(ks[3], (L, 6 * D), jnp.float32),
        "norm_gains": gain(ks[4], (L, 4, D)),
        "w_in": nrm(ks[5], (L, D, IN_COLS), D),
        "w_gate_up": nrm(ks[6], (L, GLA_GATE_RANK, GLA_K), GLA_GATE_RANK),
        "b_gate": 0.1 * jax.random.normal(ks[7], (L, GLA_K), jnp.float32),
        "gla_norm_gain": gain(ks[8], (L, GLA_HEADS, GLA_DV)),
        "w_gla_o": nrm(ks[9], (L, GLA_V, D), GLA_V),
        "w_sb_o": nrm(ks[10], (L, SB_W, D), SB_W),
        "w_out": nrm(ks[11], (L, D, D), D),
        "w_ff1": nrm(ks[12], (L, D, D_FF), D),
        "w_ff2": nrm(ks[13], (L, D_FF, D), D_FF),
    }


def reference(x, c, w_ada, b_ada, norm_gains, w_in, w_gate_up, b_gate, gla_norm_gain,
              w_gla_o, w_sb_o, w_out, w_ff1, w_ff2):
    split_idx = [int(i) for i in np.cumsum(IN_SPLITS)[:-1]]
    c_act = jax.nn.silu(c)
    for l in range(DEPTH):
        mod = (c_act @ w_ada[l] + b_ada[l])[:, None, :]
        sh1, sc1, g1, sh2, sc2, g2 = jnp.split(mod, 6, axis=-1)
        ng = norm_gains[l]

        h = rmsnorm(x, ng[0]) * (1.0 + sc1) + sh1
        proj = h @ w_in[l]
        (q_a, k_a, v_a, r_a, a_low, q_b, k_b, v_b,
         gate_a, gate_b) = jnp.split(proj, split_idx, axis=-1)
        y_a = gla_branch(q_a, k_a, v_a, r_a, a_low, w_gate_up[l], b_gate[l], gla_norm_gain[l]) @ w_gla_o[l]
        y_b = sb_branch(q_b, k_b, v_b) @ w_sb_o[l]
        mixed = jax.nn.sigmoid(gate_a) * y_a + jax.nn.sigmoid(gate_b) * y_b
        x = x + g1 * rmsnorm(mixed @ w_out[l], ng[1])

        h = rmsnorm(x, ng[2]) * (1.0 + sc2) + sh2
        f = jnp.square(jax.nn.relu(h @ w_ff1[l])) @ w_ff2[l]
        x = x + g2 * rmsnorm(f, ng[3])
    return x
```

```python
import functools

import numpy as np
import jax
import jax.numpy as jnp
from jax import lax
from jax.experimental import pallas as pl
from jax.experimental.pallas import tpu as pltpu

F32 = jnp.float32
BF16 = jnp.bfloat16

D_MODEL = 2048
BATCH = 4
SEQ = 2048
DEPTH = 4
ROWS = BATCH * SEQ

CHUNK = 64
GLA_HEADS = 4
GLA_DK = 256
GLA_DV = 512
GLA_K = GLA_HEADS * GLA_DK
GLA_V = GLA_HEADS * GLA_DV
GLA_GATE_RANK = 16
GLA_GATE_TAU = 16.0
SB_HEADS = 16
SB_DH = 128
SB_W = SB_HEADS * SB_DH
D_FF = 4 * D_MODEL
EPS = 1e-6

LANES = 128
RANK_PAD = LANES
A_LOW_START = 2 * GLA_K + 2 * GLA_V
A_LOW_END = A_LOW_START + GLA_GATE_RANK

P_COLS = 2 * GLA_K + 2 * GLA_V + 3 * SB_W + 2 * D_MODEL
COL_QA = 0
COL_KA = GLA_K
COL_VA = 2 * GLA_K
COL_RA = 2 * GLA_K + GLA_V
COL_QB = 2 * GLA_K + 2 * GLA_V
COL_KB = COL_QB + SB_W
COL_VB = COL_KB + SB_W
COL_GA = COL_VB + SB_W
COL_GB = COL_GA + D_MODEL

VMEM_LIMIT = 56 * 1024 * 1024


def _cparams(sem):
    return pltpu.CompilerParams(dimension_semantics=sem, vmem_limit_bytes=VMEM_LIMIT)


def _softplus(z):
    return jnp.maximum(z, 0.0) + jnp.log(1.0 + jnp.exp(-jnp.abs(z)))


def _split_bf16(a):
    hi = a.astype(BF16)
    lo = (a - hi.astype(F32)).astype(BF16)
    return hi, lo


ADA_TN = 1536


def _ada_kernel(c_ref, w_ref, b_ref, o_ref):
    c = c_ref[...]
    ca = (c * jax.nn.sigmoid(c)).astype(BF16)
    o_ref[0] = jnp.dot(ca, w_ref[0].astype(BF16), preferred_element_type=F32) + b_ref[0]


def _ada(c_pad, w_ada, b_ada3):
    n = 6 * D_MODEL
    return pl.pallas_call(
        _ada_kernel,
        out_shape=jax.ShapeDtypeStruct((DEPTH, 8, n), F32),
        grid=(DEPTH, n // ADA_TN),
        in_specs=[
            pl.BlockSpec((8, D_MODEL), lambda l, j: (0, 0)),
            pl.BlockSpec((1, D_MODEL, ADA_TN), lambda l, j: (l, 0, j)),
            pl.BlockSpec((1, 1, ADA_TN), lambda l, j: (l, 0, j)),
        ],
        out_specs=pl.BlockSpec((1, 8, ADA_TN), lambda l, j: (l, 0, j)),
        compiler_params=_cparams(("arbitrary", "arbitrary")),
        name="ada",
    )(c_pad, w_ada, b_ada3)


PROJ_TM = 1024
PROJ_TN = 1024


def _modulated_norm(x, gain, shift, scale):
    ms = jnp.mean(x * x, axis=-1, keepdims=True)
    y = x * lax.rsqrt(ms + EPS) * gain
    return y * (1.0 + scale) + shift


def _proj_kernel(x_ref, mod_ref, ng_ref, w_ref, wa_ref, p_ref, al_ref, h_ref):
    @pl.when(pl.program_id(1) == 0)
    def _():
        h = _modulated_norm(x_ref[...], ng_ref[0, 0:1, :], mod_ref[0, 0:1, :], mod_ref[0, 1:2, :])
        hb = h.astype(BF16)
        h_ref[...] = hb
        al_ref[...] = jnp.dot(hb, wa_ref[...], preferred_element_type=F32)

    p_ref[...] = jnp.dot(h_ref[...], w_ref[...], preferred_element_type=F32).astype(BF16)


def _proj(x2, mod_l, ng_l, w_main, w_alow):
    tm, tn = PROJ_TM, PROJ_TN
    return pl.pallas_call(
        _proj_kernel,
        out_shape=(jax.ShapeDtypeStruct((ROWS, P_COLS), BF16),
                   jax.ShapeDtypeStruct((ROWS, RANK_PAD), F32)),
        grid=(ROWS // tm, P_COLS // tn),
        in_specs=[
            pl.BlockSpec((tm, D_MODEL), lambda i, j: (i, 0)),
            pl.BlockSpec((1, 6, D_MODEL), lambda i, j: (i * tm // SEQ, 0, 0)),
            pl.BlockSpec((1, 4, D_MODEL), lambda i, j: (0, 0, 0)),
            pl.BlockSpec((D_MODEL, tn), lambda i, j: (0, j)),
            pl.BlockSpec((D_MODEL, RANK_PAD), lambda i, j: (0, 0)),
        ],
        out_specs=(pl.BlockSpec((tm, tn), lambda i, j: (i, j)),
                   pl.BlockSpec((tm, RANK_PAD), lambda i, j: (i, 0))),
        scratch_shapes=[pltpu.VMEM((tm, D_MODEL), BF16)],
        compiler_params=_cparams(("arbitrary", "arbitrary")),
        name="proj",
    )(x2, mod_l, ng_l, w_main, w_alow)


GLA_T = 512


def _gla_kernel(q_ref, k_ref, v_ref, r_ref, al_ref, wgu_ref, bg_ref, gn_ref, tri_ref,
                o_ref, st_ref):
    @pl.when(pl.program_id(2) == 0)
    def _():
        st_ref[...] = jnp.zeros_like(st_ref)

    a = jnp.dot(al_ref[...].astype(BF16), wgu_ref[...], preferred_element_type=F32) + bg_ref[...]
    log_a = -_softplus(-a) * (1.0 / GLA_GATE_TAU)
    tri = tri_ref[...]
    gain = gn_ref[0]

    for j in range(GLA_T // CHUNK):
        rows = slice(j * CHUNK, (j + 1) * CHUNK)
        hi, lo = _split_bf16(log_a[rows])
        g = (jnp.dot(tri, hi, preferred_element_type=F32)
             + jnp.dot(tri, lo, preferred_element_type=F32))
        g_tot = g[CHUNK - 1:CHUNK, :]
        kd = (k_ref[rows, :].astype(F32) * jnp.exp(g_tot - g)).astype(BF16)
        kv_t = lax.dot_general(v_ref[rows, :], kd, (((0,), (0,)), ((), ())),
                               preferred_element_type=F32)
        st = st_ref[...] * jnp.exp(g_tot) + kv_t
        st_ref[...] = st
        o = lax.dot_general(q_ref[rows, :], st.astype(BF16), (((1,), (1,)), ((), ())),
                            preferred_element_type=F32) * (GLA_DK ** -0.5)
        ms = jnp.mean(o * o, axis=-1, keepdims=True)
        on = o * lax.rsqrt(ms + EPS) * gain
        r = r_ref[rows, :].astype(F32)
        o_ref[rows, :] = (on * (r * jax.nn.sigmoid(r))).astype(BF16)


def _gla(p, a_low, wgu, bg, gn, tri):
    t = GLA_T
    nt = SEQ // t

    def row(b, h, i):
        return b * nt + i

    return pl.pallas_call(
        _gla_kernel,
        out_shape=jax.ShapeDtypeStruct((ROWS, GLA_V), BF16),
        grid=(BATCH, GLA_HEADS, nt),
        in_specs=[
            pl.BlockSpec((t, GLA_DK), lambda b, h, i: (row(b, h, i), COL_QA // GLA_DK + h)),
            pl.BlockSpec((t, GLA_DK), lambda b, h, i: (row(b, h, i), COL_KA // GLA_DK + h)),
            pl.BlockSpec((t, GLA_DV), lambda b, h, i: (row(b, h, i), COL_VA // GLA_DV + h)),
            pl.BlockSpec((t, GLA_DV), lambda b, h, i: (row(b, h, i), COL_RA // GLA_DV + h)),
            pl.BlockSpec((t, RANK_PAD), lambda b, h, i: (row(b, h, i), 0)),
            pl.BlockSpec((RANK_PAD, GLA_DK), lambda b, h, i: (0, h)),
            pl.BlockSpec((1, GLA_DK), lambda b, h, i: (0, h)),
            pl.BlockSpec((1, 1, GLA_DV), lambda b, h, i: (h, 0, 0)),
            pl.BlockSpec((CHUNK, CHUNK), lambda b, h, i: (0, 0)),
        ],
        out_specs=pl.BlockSpec((t, GLA_DV), lambda b, h, i: (row(b, h, i), h)),
        scratch_shapes=[pltpu.VMEM((GLA_DV, GLA_DK), F32)],
        compiler_params=_cparams(("arbitrary", "arbitrary", "arbitrary")),
        name="gla",
    )(p, p, p, p, a_low, wgu, bg, gn, tri)


SB_TQ = 256
SB_TK = 128


def _sb_kernel(q_ref, k_ref, v_ref, tt_ref, o_ref, acc_ref, carry_ref):
    qi = pl.program_id(2)
    q = q_ref[...]
    tt = tt_ref[...]
    scale = SB_DH ** -0.5
    acc_ref[...] = jnp.zeros_like(acc_ref)
    carry_ref[...] = jnp.zeros_like(carry_ref)

    def block(kb, past):
        ks = pl.multiple_of(kb * SB_TK, SB_TK)
        z = lax.dot_general(q, k_ref[pl.ds(ks, SB_TK), :], (((1,), (1,)), ((), ())),
                            preferred_element_type=F32) * scale
        sp = _softplus(z)
        if past is not None:
            sp = jnp.where(past, sp, 0.0)
        hi, lo = _split_bf16(sp)
        cs = jnp.dot(jnp.concatenate([hi, lo], axis=1), tt, preferred_element_type=F32)
        carry = carry_ref[...]
        w = jnp.exp(z - (cs[:, :SB_TK] + carry))
        if past is not None:
            w = jnp.where(past, w, 0.0)
        acc_ref[...] += jnp.dot(w.astype(BF16), v_ref[pl.ds(ks, SB_TK), :],
                                preferred_element_type=F32)
        carry_ref[...] = carry + cs[:, SB_TK:]

    n_diag = SB_TQ // SB_TK
    t_idx = lax.broadcasted_iota(jnp.int32, (SB_TQ, SB_TK), 0)
    s_idx = lax.broadcasted_iota(jnp.int32, (SB_TQ, SB_TK), 1)
    for d in range(n_diag - 1, -1, -1):
        block(qi * n_diag + d, (s_idx + d * SB_TK) < t_idx)

    n_full = qi * n_diag

    def body(i, _):
        block(n_full - 1 - i, None)
        return 0

    lax.fori_loop(0, n_full, body, 0)
    o_ref[...] = acc_ref[...].astype(BF16)


def _sb(p, tt):
    nq = SEQ // SB_TQ
    return pl.pallas_call(
        _sb_kernel,
        out_shape=jax.ShapeDtypeStruct((ROWS, SB_W), BF16),
        grid=(BATCH, SB_HEADS, nq),
        in_specs=[
            pl.BlockSpec((SB_TQ, SB_DH), lambda b, h, i: (b * nq + i, COL_QB // SB_DH + h)),
            pl.BlockSpec((SEQ, SB_DH), lambda b, h, i: (b, COL_KB // SB_DH + h)),
            pl.BlockSpec((SEQ, SB_DH), lambda b, h, i: (b, COL_VB // SB_DH + h)),
            pl.BlockSpec((2 * SB_TK, 2 * SB_TK), lambda b, h, i: (0, 0)),
        ],
        out_specs=pl.BlockSpec((SB_TQ, SB_DH), lambda b, h, i: (b * nq + i, h)),
        scratch_shapes=[pltpu.VMEM((SB_TQ, SB_DH), F32), pltpu.VMEM((SB_TQ, SB_TK), F32)],
        compiler_params=_cparams(("arbitrary", "arbitrary", "arbitrary")),
        name="sb",
    )(p, p, p, tt)


MIX_TM = 1024
MIX_TN = 512


def _mix_kernel(oa_ref, ob_ref, wa_ref, wb_ref, ga_ref, gb_ref, o_ref):
    ya = jnp.dot(oa_ref[...], wa_ref[...], preferred_element_type=F32)
    yb = jnp.dot(ob_ref[...], wb_ref[...], preferred_element_type=F32)
    ga = jax.nn.sigmoid(ga_ref[...].astype(F32))
    gb = jax.nn.sigmoid(gb_ref[...].astype(F32))
    o_ref[...] = (ga * ya + gb * yb).astype(BF16)


def _mix(o_gla, o_sb, w_gla_o, w_sb_o, p):
    tm, tn = MIX_TM, MIX_TN
    return pl.pallas_call(
        _mix_kernel,
        out_shape=jax.ShapeDtypeStruct((ROWS, D_MODEL), BF16),
        grid=(ROWS // tm, D_MODEL // tn),
        in_specs=[
            pl.BlockSpec((tm, GLA_V), lambda i, j: (i, 0)),
            pl.BlockSpec((tm, SB_W), lambda i, j: (i, 0)),
            pl.BlockSpec((GLA_V, tn), lambda i, j: (0, j)),
            pl.BlockSpec((SB_W, tn), lambda i, j: (0, j)),
            pl.BlockSpec((tm, tn), lambda i, j: (i, COL_GA // tn + j)),
            pl.BlockSpec((tm, tn), lambda i, j: (i, COL_GB // tn + j)),
        ],
        out_specs=pl.BlockSpec((tm, tn), lambda i, j: (i, j)),
        compiler_params=_cparams(("arbitrary", "arbitrary")),
        name="mix",
    )(o_gla, o_sb, w_gla_o, w_sb_o, p, p)


OUT_TM = 512


def _out_kernel(m_ref, w_ref, x_ref, mod_ref, ng_ref, o_ref):
    y = jnp.dot(m_ref[...], w_ref[...], preferred_element_type=F32)
    ms = jnp.mean(y * y, axis=-1, keepdims=True)
    yn = y * lax.rsqrt(ms + EPS) * ng_ref[0, 1:2, :]
    o_ref[...] = x_ref[...] + mod_ref[0, 2:3, :] * yn


def _out(mixed, w_out, x2, mod_l, ng_l):
    tm = OUT_TM
    return pl.pallas_call(
        _out_kernel,
        out_shape=jax.ShapeDtypeStruct((ROWS, D_MODEL), F32),
        grid=(ROWS // tm,),
        in_specs=[
            pl.BlockSpec((tm, D_MODEL), lambda i: (i, 0)),
            pl.BlockSpec((D_MODEL, D_MODEL), lambda i: (0, 0)),
            pl.BlockSpec((tm, D_MODEL), lambda i: (i, 0)),
            pl.BlockSpec((1, 6, D_MODEL), lambda i: (i * tm // SEQ, 0, 0)),
            pl.BlockSpec((1, 4, D_MODEL), lambda i: (0, 0, 0)),
        ],
        out_specs=pl.BlockSpec((tm, D_MODEL), lambda i: (i, 0)),
        compiler_params=_cparams(("arbitrary",)),
        name="out",
    )(mixed, w_out, x2, mod_l, ng_l)


FFN_TM = 512
FFN_TF = 1024


def _ffn_kernel(x_ref, mod_ref, ng_ref, w1_ref, w2_ref, o_ref, h_ref, acc_ref):
    f = pl.program_id(1)

    @pl.when(f == 0)
    def _():
        h = _modulated_norm(x_ref[...], ng_ref[0, 2:3, :], mod_ref[0, 3:4, :], mod_ref[0, 4:5, :])
        h_ref[...] = h.astype(BF16)
        acc_ref[...] = jnp.zeros_like(acc_ref)

    a = jnp.dot(h_ref[...], w1_ref[...], preferred_element_type=F32)
    a = jnp.square(jnp.maximum(a, 0.0)).astype(BF16)
    acc_ref[...] += jnp.dot(a, w2_ref[...], preferred_element_type=F32)

    @pl.when(f == pl.num_programs(1) - 1)
    def _():
        y = acc_ref[...]
        ms = jnp.mean(y * y, axis=-1, keepdims=True)
        yn = y * lax.rsqrt(ms + EPS) * ng_ref[0, 3:4, :]
        o_ref[...] = x_ref[...] + mod_ref[0, 5:6, :] * yn


def _ffn(x2, mod_l, ng_l, w1, w2):
    tm, tf = FFN_TM, FFN_TF
    return pl.pallas_call(
        _ffn_kernel,
        out_shape=jax.ShapeDtypeStruct((ROWS, D_MODEL), F32),
        grid=(ROWS // tm, D_FF // tf),
        in_specs=[
            pl.BlockSpec((tm, D_MODEL), lambda i, f: (i, 0)),
            pl.BlockSpec((1, 6, D_MODEL), lambda i, f: (i * tm // SEQ, 0, 0)),
            pl.BlockSpec((1, 4, D_MODEL), lambda i, f: (0, 0, 0)),
            pl.BlockSpec((D_MODEL, tf), lambda i, f: (0, f)),
            pl.BlockSpec((tf, D_MODEL), lambda i, f: (f, 0)),
        ],
        out_specs=pl.BlockSpec((tm, D_MODEL), lambda i, f: (i, 0)),
        scratch_shapes=[pltpu.VMEM((tm, D_MODEL), BF16), pltpu.VMEM((tm, D_MODEL), F32)],
        compiler_params=_cparams(("arbitrary", "arbitrary")),
        name="ffn",
    )(x2, mod_l, ng_l, w1, w2)


def _cumsum_constants():
    j = np.arange(CHUNK)
    tri = (j[:, None] >= j[None, :]).astype(np.float32)
    jj = np.arange(2 * SB_TK) % SB_TK
    s = np.arange(2 * SB_TK)
    tt = ((jj[:, None] >= s[None, :]) | (s[None, :] >= SB_TK)).astype(np.float32)
    return jnp.asarray(tri, BF16), jnp.asarray(tt, BF16)


def kernel(x, c, w_ada, b_ada, norm_gains, w_in, w_gate_up, b_gate, gla_norm_gain,
           w_gla_o, w_sb_o, w_out, w_ff1, w_ff2):
    tri, tt = _cumsum_constants()

    c_pad = jnp.pad(c, ((0, 8 - BATCH), (0, 0)))
    mod = _ada(c_pad, w_ada, b_ada.reshape(DEPTH, 1, 6 * D_MODEL))
    mod = mod[:, :BATCH, :].reshape(DEPTH, BATCH, 6, D_MODEL)

    w_main = jnp.concatenate([w_in[:, :, :A_LOW_START], w_in[:, :, A_LOW_END:]], axis=-1).astype(BF16)
    w_alow = jnp.pad(w_in[:, :, A_LOW_START:A_LOW_END],
                     ((0, 0), (0, 0), (0, RANK_PAD - GLA_GATE_RANK))).astype(BF16)
    wgu = jnp.pad(w_gate_up, ((0, 0), (0, RANK_PAD - GLA_GATE_RANK), (0, 0))).astype(BF16)
    bg = b_gate.reshape(DEPTH, 1, GLA_K)
    gn = gla_norm_gain.reshape(DEPTH, GLA_HEADS, 1, GLA_DV)
    w_gla_o_b = w_gla_o.astype(BF16)
    w_sb_o_b = w_sb_o.astype(BF16)
    w_out_b = w_out.astype(BF16)
    w_ff1_b = w_ff1.astype(BF16)
    w_ff2_b = w_ff2.astype(BF16)

    x2 = x.reshape(ROWS, D_MODEL)
    for l in range(DEPTH):
        mod_l = mod[l]
        ng_l = norm_gains[l].reshape(1, 4, D_MODEL)
        p, a_low = _proj(x2, mod_l, ng_l, w_main[l], w_alow[l])
        o_gla = _gla(p, a_low, wgu[l], bg[l], gn[l], tri)
        o_sb = _sb(p, tt)
        mixed = _mix(o_gla, o_sb, w_gla_o_b[l], w_sb_o_b[l], p)
        x2 = _out(mixed, w_out_b[l], x2, mod_l, ng_l)
        x2 = _ffn(x2, mod_l, ng_l, w_ff1_b[l], w_ff2_b[l])
    return x2.reshape(BATCH, SEQ, D_MODEL)
```

```python
import functools

import numpy as np
import jax
import jax.numpy as jnp
from jax import lax
from jax.experimental import pallas as pl
from jax.experimental.pallas import tpu as pltpu

F32 = jnp.float32
BF16 = jnp.bfloat16

D_MODEL = 2048
BATCH = 4
SEQ = 2048
DEPTH = 4
ROWS = BATCH * SEQ

CHUNK = 64
GLA_HEADS = 4
GLA_DK = 256
GLA_DV = 512
GLA_K = GLA_HEADS * GLA_DK
GLA_V = GLA_HEADS * GLA_DV
GLA_GATE_RANK = 16
GLA_GATE_TAU = 16.0
SB_HEADS = 16
SB_DH = 128
SB_W = SB_HEADS * SB_DH
D_FF = 4 * D_MODEL
EPS = 1e-6

LOG2E = 1.4426950408889634
LANES = 128
RANK_PAD = LANES
A_LOW_START = 2 * GLA_K + 2 * GLA_V
A_LOW_END = A_LOW_START + GLA_GATE_RANK

P_COLS = 2 * GLA_K + 2 * GLA_V + 3 * SB_W + 2 * D_MODEL
COL_QA = 0
COL_KA = GLA_K
COL_VA = 2 * GLA_K
COL_RA = 2 * GLA_K + GLA_V
COL_QB = 2 * GLA_K + 2 * GLA_V
COL_KB = COL_QB + SB_W
COL_VB = COL_KB + SB_W
COL_GA = COL_VB + SB_W
COL_GB = COL_GA + D_MODEL

VMEM_LIMIT = 56 * 1024 * 1024


def _cparams(sem):
    return pltpu.CompilerParams(dimension_semantics=sem, vmem_limit_bytes=VMEM_LIMIT)


def _softplus(z):
    return jnp.maximum(z, 0.0) + jnp.log(1.0 + jnp.exp(-jnp.abs(z)))


def _split_bf16(a):
    hi = a.astype(BF16)
    lo = (a - hi.astype(F32)).astype(BF16)
    return hi, lo


ADA_TN = 1536


def _ada_kernel(c_ref, w_ref, b_ref, o_ref):
    c = c_ref[...]
    ca = (c * jax.nn.sigmoid(c)).astype(BF16)
    o_ref[0] = jnp.dot(ca, w_ref[0].astype(BF16), preferred_element_type=F32) + b_ref[0]


def _ada(c_pad, w_ada, b_ada3):
    n = 6 * D_MODEL
    return pl.pallas_call(
        _ada_kernel,
        out_shape=jax.ShapeDtypeStruct((DEPTH, 8, n), F32),
        grid=(DEPTH, n // ADA_TN),
        in_specs=[
            pl.BlockSpec((8, D_MODEL), lambda l, j: (0, 0)),
            pl.BlockSpec((1, D_MODEL, ADA_TN), lambda l, j: (l, 0, j)),
            pl.BlockSpec((1, 1, ADA_TN), lambda l, j: (l, 0, j)),
        ],
        out_specs=pl.BlockSpec((1, 8, ADA_TN), lambda l, j: (l, 0, j)),
        compiler_params=_cparams(("arbitrary", "arbitrary")),
        name="ada",
    )(c_pad, w_ada, b_ada3)


PROJ_TM = 1024
PROJ_TN = 1024


def _modulated_norm(x, gain, shift, scale):
    ms = jnp.mean(x * x, axis=-1, keepdims=True)
    y = x * lax.rsqrt(ms + EPS) * gain
    return y * (1.0 + scale) + shift


def _proj_kernel(x_ref, mod_ref, ng_ref, w_ref, wa_ref, p_ref, al_ref, h_ref):
    @pl.when(pl.program_id(1) == 0)
    def _():
        h = _modulated_norm(x_ref[...], ng_ref[0:1, :], mod_ref[0:1, :], mod_ref[1:2, :])
        hb = h.astype(BF16)
        h_ref[...] = hb
        al_ref[...] = jnp.dot(hb, wa_ref[...], preferred_element_type=F32)

    p_ref[...] = jnp.dot(h_ref[...], w_ref[...], preferred_element_type=F32).astype(BF16)


def _mod_spec(l, tm):
    return pl.BlockSpec((None, None, 6, D_MODEL), lambda i, *_: (l, i * tm // SEQ, 0, 0))


def _gain_spec(l):
    return pl.BlockSpec((None, 4, D_MODEL), lambda *_: (l, 0, 0))


def _proj(l, x2, mod, norm_gains, w_main, w_alow):
    tm, tn = PROJ_TM, PROJ_TN
    return pl.pallas_call(
        _proj_kernel,
        out_shape=(jax.ShapeDtypeStruct((ROWS, P_COLS), BF16),
                   jax.ShapeDtypeStruct((ROWS, RANK_PAD), F32)),
        grid=(ROWS // tm, P_COLS // tn),
        in_specs=[
            pl.BlockSpec((tm, D_MODEL), lambda i, j: (i, 0)),
            _mod_spec(l, tm),
            _gain_spec(l),
            pl.BlockSpec((None, D_MODEL, tn), lambda i, j: (l, 0, j)),
            pl.BlockSpec((None, D_MODEL, RANK_PAD), lambda i, j: (l, 0, 0)),
        ],
        out_specs=(pl.BlockSpec((tm, tn), lambda i, j: (i, j)),
                   pl.BlockSpec((tm, RANK_PAD), lambda i, j: (i, 0))),
        scratch_shapes=[pltpu.VMEM((tm, D_MODEL), BF16)],
        compiler_params=_cparams(("arbitrary", "arbitrary")),
        name="proj",
    )(x2, mod, norm_gains, w_main, w_alow)


GLA_T = 512


def _gla_kernel(q_ref, k_ref, v_ref, r_ref, al_ref, wgu_ref, bg_ref, gn_ref, tri_ref,
                o_ref, st_ref):
    @pl.when(pl.program_id(2) == 0)
    def _():
        st_ref[...] = jnp.zeros_like(st_ref)

    a = jnp.dot(al_ref[...].astype(BF16), wgu_ref[...], preferred_element_type=F32) + bg_ref[...]
    log_a = -_softplus(-a) * (1.0 / GLA_GATE_TAU)
    tri = tri_ref[...]
    gain = gn_ref[...]

    for j in range(GLA_T // CHUNK):
        rows = slice(j * CHUNK, (j + 1) * CHUNK)
        hi, lo = _split_bf16(log_a[rows])
        g = (jnp.dot(tri, hi, preferred_element_type=F32)
             + jnp.dot(tri, lo, preferred_element_type=F32))
        g_tot = g[CHUNK - 1:CHUNK, :]
        kd = (k_ref[rows, :].astype(F32) * jnp.exp(g_tot - g)).astype(BF16)
        kv_t = lax.dot_general(v_ref[rows, :], kd, (((0,), (0,)), ((), ())),
                               preferred_element_type=F32)
        st = st_ref[...] * jnp.exp(g_tot) + kv_t
        st_ref[...] = st
        o = lax.dot_general(q_ref[rows, :], st.astype(BF16), (((1,), (1,)), ((), ())),
                            preferred_element_type=F32) * (GLA_DK ** -0.5)
        ms = jnp.mean(o * o, axis=-1, keepdims=True)
        on = o * lax.rsqrt(ms + EPS) * gain
        r = r_ref[rows, :].astype(F32)
        o_ref[rows, :] = (on * (r * jax.nn.sigmoid(r))).astype(BF16)


def _gla(l, p, a_low, wgu, bg, gn, tri):
    t = GLA_T
    nt = SEQ // t

    def row(b, h, i):
        return b * nt + i

    return pl.pallas_call(
        _gla_kernel,
        out_shape=jax.ShapeDtypeStruct((ROWS, GLA_V), BF16),
        grid=(BATCH, GLA_HEADS, nt),
        in_specs=[
            pl.BlockSpec((t, GLA_DK), lambda b, h, i: (row(b, h, i), COL_QA // GLA_DK + h)),
            pl.BlockSpec((t, GLA_DK), lambda b, h, i: (row(b, h, i), COL_KA // GLA_DK + h)),
            pl.BlockSpec((t, GLA_DV), lambda b, h, i: (row(b, h, i), COL_VA // GLA_DV + h)),
            pl.BlockSpec((t, GLA_DV), lambda b, h, i: (row(b, h, i), COL_RA // GLA_DV + h)),
            pl.BlockSpec((t, RANK_PAD), lambda b, h, i: (row(b, h, i), 0)),
            pl.BlockSpec((None, RANK_PAD, GLA_DK), lambda b, h, i: (l, 0, h)),
            pl.BlockSpec((None, 1, GLA_DK), lambda b, h, i: (l, 0, h)),
            pl.BlockSpec((None, None, 1, GLA_DV), lambda b, h, i: (l, h, 0, 0)),
            pl.BlockSpec((CHUNK, CHUNK), lambda b, h, i: (0, 0)),
        ],
        out_specs=pl.BlockSpec((t, GLA_DV), lambda b, h, i: (row(b, h, i), h)),
        scratch_shapes=[pltpu.VMEM((GLA_DV, GLA_DK), F32)],
        compiler_params=_cparams(("arbitrary", "arbitrary", "arbitrary")),
        name="gla",
    )(p, p, p, p, a_low, wgu, bg, gn, tri)


SB_TQ = 256
SB_TK = 256
SB_HB = 4
SB_HW = SB_HB * SB_DH
SB_MASKED = -1e30


def _sb_kernel(q_ref, k_ref, v_ref, tt_ref, o_ref, acc_ref, carry_ref, sp_ref, lb_ref):
    qi = pl.program_id(2)
    tt = tt_ref[...]
    scale = SB_DH ** -0.5 * LOG2E
    acc_ref[...] = jnp.zeros_like(acc_ref)
    carry_ref[...] = jnp.zeros_like(carry_ref)

    def scores(kb, slot, past=None):
        ks = pl.multiple_of(kb * SB_TK, SB_TK)
        for h in range(SB_HB):
            cols = slice(h * SB_DH, (h + 1) * SB_DH)
            z = lax.dot_general(q_ref[:, cols], k_ref[pl.ds(ks, SB_TK), cols],
                                (((1,), (1,)), ((), ())), preferred_element_type=F32) * scale
            sp = jnp.maximum(z, 0.0) + jnp.log2(1.0 + jnp.exp2(-jnp.abs(z)))
            log_beta = z - sp
            if past is not None:
                sp = jnp.where(past, sp, 0.0)
                log_beta = jnp.where(past, log_beta, SB_MASKED)
            sp_ref[slot, h] = sp.astype(BF16)
            lb_ref[slot, h] = log_beta

    def weights(kb, slot):
        ks = pl.multiple_of(kb * SB_TK, SB_TK)
        for h in range(SB_HB):
            cols = slice(h * SB_DH, (h + 1) * SB_DH)
            cs = jnp.dot(sp_ref[slot, h], tt, preferred_element_type=F32)
            carry = carry_ref[h]
            w = jnp.exp2(lb_ref[slot, h] - (cs[:, :SB_TK] + carry))
            acc_ref[h] += jnp.dot(w.astype(BF16), v_ref[pl.ds(ks, SB_TK), cols],
                                  preferred_element_type=F32)
            carry_ref[h] = carry + cs[:, SB_TK:]

    t_idx = lax.broadcasted_iota(jnp.int32, (SB_TQ, SB_TK), 0)
    s_idx = lax.broadcasted_iota(jnp.int32, (SB_TQ, SB_TK), 1)
    scores(qi, 0, s_idx < t_idx)

    def pair(i, _):
        kb = qi - 2 * i
        weights(kb, 0)
        scores(kb - 1, 1)
        weights(kb - 1, 1)
        scores(kb - 2, 0)
        return 0

    lax.fori_loop(0, qi // 2, pair, 0)

    @pl.when(qi % 2 == 1)
    def _():
        weights(1, 0)
        scores(0, 1)
        weights(0, 1)

    @pl.when(qi % 2 == 0)
    def _():
        weights(0, 0)

    for h in range(SB_HB):
        o_ref[:, h * SB_DH:(h + 1) * SB_DH] = acc_ref[h].astype(BF16)


def _sb(p, tt):
    nq = SEQ // SB_TQ
    return pl.pallas_call(
        _sb_kernel,
        out_shape=jax.ShapeDtypeStruct((ROWS, SB_W), BF16),
        grid=(BATCH, SB_HEADS // SB_HB, nq),
        in_specs=[
            pl.BlockSpec((SB_TQ, SB_HW), lambda b, h, i: (b * nq + i, COL_QB // SB_HW + h)),
            pl.BlockSpec((SEQ, SB_HW), lambda b, h, i: (b, COL_KB // SB_HW + h)),
            pl.BlockSpec((SEQ, SB_HW), lambda b, h, i: (b, COL_VB // SB_HW + h)),
            pl.BlockSpec((SB_TK, 2 * SB_TK), lambda b, h, i: (0, 0)),
        ],
        out_specs=pl.BlockSpec((SB_TQ, SB_HW), lambda b, h, i: (b * nq + i, h)),
        scratch_shapes=[pltpu.VMEM((SB_HB, SB_TQ, SB_DH), F32),
                        pltpu.VMEM((SB_HB, SB_TQ, SB_TK), F32),
                        pltpu.VMEM((2, SB_HB, SB_TQ, SB_TK), BF16),
                        pltpu.VMEM((2, SB_HB, SB_TQ, SB_TK), F32)],
        compiler_params=_cparams(("arbitrary", "arbitrary", "arbitrary")),
        name="sb",
    )(p, p, p, tt)


MIX_TM = 1024
MIX_TN = 512


def _mix_kernel(oa_ref, ob_ref, wa_ref, wb_ref, ga_ref, gb_ref, o_ref):
    ya = jnp.dot(oa_ref[...], wa_ref[...], preferred_element_type=F32)
    yb = jnp.dot(ob_ref[...], wb_ref[...], preferred_element_type=F32)
    ga = jax.nn.sigmoid(ga_ref[...].astype(F32))
    gb = jax.nn.sigmoid(gb_ref[...].astype(F32))
    o_ref[...] = (ga * ya + gb * yb).astype(BF16)


def _mix(l, o_gla, o_sb, w_gla_o, w_sb_o, p):
    tm, tn = MIX_TM, MIX_TN
    return pl.pallas_call(
        _mix_kernel,
        out_shape=jax.ShapeDtypeStruct((ROWS, D_MODEL), BF16),
        grid=(ROWS // tm, D_MODEL // tn),
        in_specs=[
            pl.BlockSpec((tm, GLA_V), lambda i, j: (i, 0)),
            pl.BlockSpec((tm, SB_W), lambda i, j: (i, 0)),
            pl.BlockSpec((None, GLA_V, tn), lambda i, j: (l, 0, j)),
            pl.BlockSpec((None, SB_W, tn), lambda i, j: (l, 0, j)),
            pl.BlockSpec((tm, tn), lambda i, j: (i, COL_GA // tn + j)),
            pl.BlockSpec((tm, tn), lambda i, j: (i, COL_GB // tn + j)),
        ],
        out_specs=pl.BlockSpec((tm, tn), lambda i, j: (i, j)),
        compiler_params=_cparams(("arbitrary", "arbitrary")),
        name="mix",
    )(o_gla, o_sb, w_gla_o, w_sb_o, p, p)


OUT_TM = 512


def _out_kernel(m_ref, w_ref, x_ref, mod_ref, ng_ref, o_ref):
    y = jnp.dot(m_ref[...], w_ref[...], preferred_element_type=F32)
    ms = jnp.mean(y * y, axis=-1, keepdims=True)
    yn = y * lax.rsqrt(ms + EPS) * ng_ref[1:2, :]
    o_ref[...] = x_ref[...] + mod_ref[2:3, :] * yn


def _out(l, mixed, w_out, x2, mod, norm_gains):
    tm = OUT_TM
    return pl.pallas_call(
        _out_kernel,
        out_shape=jax.ShapeDtypeStruct((ROWS, D_MODEL), F32),
        grid=(ROWS // tm,),
        in_specs=[
            pl.BlockSpec((tm, D_MODEL), lambda i: (i, 0)),
            pl.BlockSpec((None, D_MODEL, D_MODEL), lambda i: (l, 0, 0)),
            pl.BlockSpec((tm, D_MODEL), lambda i: (i, 0)),
            _mod_spec(l, tm),
            _gain_spec(l),
        ],
        out_specs=pl.BlockSpec((tm, D_MODEL), lambda i: (i, 0)),
        compiler_params=_cparams(("arbitrary",)),
        name="out",
    )(mixed, w_out, x2, mod, norm_gains)


FFN_TM = 512
FFN_TF = 1024


def _ffn_kernel(x_ref, mod_ref, ng_ref, w1_ref, w2_ref, o_ref, h_ref, acc_ref):
    f = pl.program_id(1)

    @pl.when(f == 0)
    def _():
        h = _modulated_norm(x_ref[...], ng_ref[2:3, :], mod_ref[3:4, :], mod_ref[4:5, :])
        h_ref[...] = h.astype(BF16)
        acc_ref[...] = jnp.zeros_like(acc_ref)

    a = jnp.dot(h_ref[...], w1_ref[...], preferred_element_type=F32)
    a = jnp.square(jnp.maximum(a, 0.0)).astype(BF16)
    acc_ref[...] += jnp.dot(a, w2_ref[...], preferred_element_type=F32)

    @pl.when(f == pl.num_programs(1) - 1)
    def _():
        y = acc_ref[...]
        ms = jnp.mean(y * y, axis=-1, keepdims=True)
        yn = y * lax.rsqrt(ms + EPS) * ng_ref[3:4, :]
        o_ref[...] = x_ref[...] + mod_ref[5:6, :] * yn


def _ffn(l, x2, mod, norm_gains, w1, w2):
    tm, tf = FFN_TM, FFN_TF
    return pl.pallas_call(
        _ffn_kernel,
        out_shape=jax.ShapeDtypeStruct((ROWS, D_MODEL), F32),
        grid=(ROWS // tm, D_FF // tf),
        in_specs=[
            pl.BlockSpec((tm, D_MODEL), lambda i, f: (i, 0)),
            _mod_spec(l, tm),
            _gain_spec(l),
            pl.BlockSpec((None, D_MODEL, tf), lambda i, f: (l, 0, f)),
            pl.BlockSpec((None, tf, D_MODEL), lambda i, f: (l, f, 0)),
        ],
        out_specs=pl.BlockSpec((tm, D_MODEL), lambda i, f: (i, 0)),
        scratch_shapes=[pltpu.VMEM((tm, D_MODEL), BF16), pltpu.VMEM((tm, D_MODEL), F32)],
        compiler_params=_cparams(("arbitrary", "arbitrary")),
        name="ffn",
    )(x2, mod, norm_gains, w1, w2)


def _cumsum_constants():
    j = np.arange(CHUNK)
    tri = (j[:, None] >= j[None, :]).astype(np.float32)
    jj = np.arange(SB_TK)
    s = np.arange(2 * SB_TK)
    tt = ((jj[:, None] > s[None, :]) | (s[None, :] >= SB_TK)).astype(np.float32)
    return jnp.asarray(tri, BF16), jnp.asarray(tt, BF16)


def kernel(x, c, w_ada, b_ada, norm_gains, w_in, w_gate_up, b_gate, gla_norm_gain,
           w_gla_o, w_sb_o, w_out, w_ff1, w_ff2):
    tri, tt = _cumsum_constants()

    c_pad = jnp.pad(c, ((0, 8 - BATCH), (0, 0)))
    mod = _ada(c_pad, w_ada, b_ada.reshape(DEPTH, 1, 6 * D_MODEL))
    mod = mod[:, :BATCH, :].reshape(DEPTH, BATCH, 6, D_MODEL)

    w_main = jnp.concatenate([w_in[:, :, :A_LOW_START], w_in[:, :, A_LOW_END:]], axis=-1).astype(BF16)
    w_alow = jnp.pad(w_in[:, :, A_LOW_START:A_LOW_END],
                     ((0, 0), (0, 0), (0, RANK_PAD - GLA_GATE_RANK))).astype(BF16)
    wgu = jnp.pad(w_gate_up, ((0, 0), (0, RANK_PAD - GLA_GATE_RANK), (0, 0))).astype(BF16)
    bg = b_gate.reshape(DEPTH, 1, GLA_K)
    gn = gla_norm_gain.reshape(DEPTH, GLA_HEADS, 1, GLA_DV)
    w_gla_o_b = w_gla_o.astype(BF16)
    w_sb_o_b = w_sb_o.astype(BF16)
    w_out_b = w_out.astype(BF16)
    w_ff1_b = w_ff1.astype(BF16)
    w_ff2_b = w_ff2.astype(BF16)

    x2 = x.reshape(ROWS, D_MODEL)
    for l in range(DEPTH):
        p, a_low = _proj(l, x2, mod, norm_gains, w_main, w_alow)
        o_gla = _gla(l, p, a_low, wgu, bg, gn, tri)
        o_sb = _sb(p, tt)
        mixed = _mix(l, o_gla, o_sb, w_gla_o_b, w_sb_o_b, p)
        x2 = _out(l, mixed, w_out_b, x2, mod, norm_gains)
        x2 = _ffn(l, x2, mod, norm_gains, w_ff1_b, w_ff2_b)
    return x2.reshape(BATCH, SEQ, D_MODEL)
```

```python
import functools

import numpy as np
import jax
import jax.numpy as jnp
from jax import lax
from jax.experimental import pallas as pl
from jax.experimental.pallas import tpu as pltpu

F32 = jnp.float32
BF16 = jnp.bfloat16

D_MODEL = 2048
BATCH = 4
SEQ = 2048
DEPTH = 4
ROWS = BATCH * SEQ

CHUNK = 64
GLA_HEADS = 4
GLA_DK = 256
GLA_DV = 512
GLA_K = GLA_HEADS * GLA_DK
GLA_V = GLA_HEADS * GLA_DV
GLA_GATE_RANK = 16
GLA_GATE_TAU = 16.0
SB_HEADS = 16
SB_DH = 128
SB_W = SB_HEADS * SB_DH
D_FF = 4 * D_MODEL
EPS = 1e-6

LOG2E = 1.4426950408889634
LANES = 128
RANK_PAD = LANES
A_LOW_START = 2 * GLA_K + 2 * GLA_V
A_LOW_END = A_LOW_START + GLA_GATE_RANK

P_COLS = 2 * GLA_K + 2 * GLA_V + 3 * SB_W + 2 * D_MODEL
COL_QA = 0
COL_KA = GLA_K
COL_VA = 2 * GLA_K
COL_RA = 2 * GLA_K + GLA_V
COL_QB = 2 * GLA_K + 2 * GLA_V
COL_KB = COL_QB + SB_W
COL_VB = COL_KB + SB_W
COL_GA = COL_VB + SB_W
COL_GB = COL_GA + D_MODEL

VMEM_LIMIT = 56 * 1024 * 1024


def _cparams(sem):
    return pltpu.CompilerParams(dimension_semantics=sem, vmem_limit_bytes=VMEM_LIMIT)


def _softplus(z):
    return jnp.maximum(z, 0.0) + jnp.log(1.0 + jnp.exp(-jnp.abs(z)))


def _neg_abs(x):
    bits = pltpu.bitcast(x, jnp.uint32) | jnp.uint32(0x80000000)
    return pltpu.bitcast(bits, F32)


def _split_bf16(a):
    hi = a.astype(BF16)
    lo = (a - hi.astype(F32)).astype(BF16)
    return hi, lo


ADA_TK = 256


def _ada_kernel(c_ref, w_ref, b_ref, o_ref):
    @pl.when(pl.program_id(1) == 0)
    def _():
        o_ref[...] = jnp.broadcast_to(b_ref[...], o_ref.shape)

    c = c_ref[...]
    ca = (c * jax.nn.sigmoid(c)).astype(BF16)
    o_ref[...] += jnp.dot(ca, w_ref[...].astype(BF16), preferred_element_type=F32)


def _ada(c_pad, w_ada, b_ada3):
    n = 6 * D_MODEL
    return pl.pallas_call(
        _ada_kernel,
        out_shape=jax.ShapeDtypeStruct((DEPTH, 8, n), F32),
        grid=(DEPTH, D_MODEL // ADA_TK),
        in_specs=[
            pl.BlockSpec((8, ADA_TK), lambda l, k: (0, k)),
            pl.BlockSpec((None, ADA_TK, n), lambda l, k: (l, k, 0)),
            pl.BlockSpec((None, 1, n), lambda l, k: (l, 0, 0)),
        ],
        out_specs=pl.BlockSpec((None, 8, n), lambda l, k: (l, 0, 0)),
        compiler_params=_cparams(("arbitrary", "arbitrary")),
        name="ada",
    )(c_pad, w_ada, b_ada3)


PROJ_TM = 1024
PROJ_TN = 2048
PACK_TN = 1024
PACK_LO_TILES = A_LOW_START // PACK_TN


def _modulated_norm(x, gain, shift, scale):
    ms = jnp.mean(x * x, axis=-1, keepdims=True)
    y = x * lax.rsqrt(ms + EPS) * gain
    return y * (1.0 + scale) + shift


def _pack_kernel(w_ref, nxt_ref, o_ref):
    j = pl.program_id(1)

    @pl.when(j < PACK_LO_TILES)
    def _():
        o_ref[...] = w_ref[...].astype(BF16)

    @pl.when(j >= PACK_LO_TILES)
    def _():
        wide = jnp.concatenate([w_ref[...], nxt_ref[...]], axis=1)
        o_ref[...] = wide[:, GLA_GATE_RANK:GLA_GATE_RANK + PACK_TN].astype(BF16)


def _pack_w_in(w_in):
    tn = PACK_TN
    return pl.pallas_call(
        _pack_kernel,
        out_shape=jax.ShapeDtypeStruct((DEPTH, D_MODEL, P_COLS), BF16),
        grid=(DEPTH, P_COLS // tn),
        in_specs=[
            pl.BlockSpec((None, D_MODEL, tn), lambda l, j: (l, 0, j)),
            pl.BlockSpec((None, D_MODEL, LANES), lambda l, j: (l, 0, (j + 1) * (tn // LANES))),
        ],
        out_specs=pl.BlockSpec((None, D_MODEL, tn), lambda l, j: (l, 0, j)),
        compiler_params=_cparams(("arbitrary", "arbitrary")),
        name="pack_w_in",
    )(w_in, w_in)


def _proj_kernel(x_ref, mod_ref, ng_ref, w_ref, wa_ref, p_ref, al_ref, h_ref):
    @pl.when(pl.program_id(1) == 0)
    def _():
        h = _modulated_norm(x_ref[...], ng_ref[0:1, :], mod_ref[0:1, :], mod_ref[1:2, :])
        hb = h.astype(BF16)
        h_ref[...] = hb
        al_ref[...] = jnp.dot(hb, wa_ref[...], preferred_element_type=F32)

    p_ref[...] = jnp.dot(h_ref[...], w_ref[...], preferred_element_type=F32).astype(BF16)


def _mod_spec(l, tm):
    return pl.BlockSpec((None, None, 6, D_MODEL), lambda i, *_: (l, i * tm // SEQ, 0, 0))


def _gain_spec(l):
    return pl.BlockSpec((None, 4, D_MODEL), lambda *_: (l, 0, 0))


def _proj(l, x2, mod, norm_gains, w_main, w_alow):
    tm, tn = PROJ_TM, PROJ_TN
    return pl.pallas_call(
        _proj_kernel,
        out_shape=(jax.ShapeDtypeStruct((ROWS, P_COLS), BF16),
                   jax.ShapeDtypeStruct((ROWS, RANK_PAD), F32)),
        grid=(ROWS // tm, P_COLS // tn),
        in_specs=[
            pl.BlockSpec((tm, D_MODEL), lambda i, j: (i, 0)),
            _mod_spec(l, tm),
            _gain_spec(l),
            pl.BlockSpec((None, D_MODEL, tn), lambda i, j: (l, 0, j)),
            pl.BlockSpec((None, D_MODEL, RANK_PAD), lambda i, j: (l, 0, 0)),
        ],
        out_specs=(pl.BlockSpec((tm, tn), lambda i, j: (i, j)),
                   pl.BlockSpec((tm, RANK_PAD), lambda i, j: (i, 0))),
        scratch_shapes=[pltpu.VMEM((tm, D_MODEL), BF16)],
        compiler_params=_cparams(("arbitrary", "arbitrary")),
        name="proj",
    )(x2, mod, norm_gains, w_main, w_alow)


GLA_T = 512


def _gla_kernel(q_ref, k_ref, v_ref, r_ref, al_ref, wgu_ref, bg_ref, gn_ref, tri_ref,
                o_ref, st_ref):
    @pl.when(pl.program_id(2) == 0)
    def _():
        st_ref[...] = jnp.zeros_like(st_ref)

    a = jnp.dot(al_ref[...].astype(BF16), wgu_ref[...], preferred_element_type=F32) + bg_ref[...]
    log_a = -_softplus(-a) * (1.0 / GLA_GATE_TAU)
    tri = tri_ref[...]
    gain = gn_ref[...]

    n_chunks = GLA_T // CHUNK

    def chunk_rows(j):
        return slice(j * CHUNK, (j + 1) * CHUNK)

    def chunk_kv(j):
        rows = chunk_rows(j)
        hi, lo = _split_bf16(log_a[rows])
        g = (jnp.dot(tri, hi, preferred_element_type=F32)
             + jnp.dot(tri, lo, preferred_element_type=F32))
        g_tot = g[CHUNK - 1:CHUNK, :]
        kd = (k_ref[rows, :].astype(F32) * jnp.exp(g_tot - g)).astype(BF16)
        kv_t = lax.dot_general(v_ref[rows, :], kd, (((0,), (0,)), ((), ())),
                               preferred_element_type=F32)
        return jnp.exp(g_tot), kv_t

    ahead = 2
    pending = [chunk_kv(j) for j in range(ahead)]
    for j in range(n_chunks):
        decay, kv_t = pending.pop(0)
        if j + ahead < n_chunks:
            pending.append(chunk_kv(j + ahead))
        rows = chunk_rows(j)
        st = st_ref[...] * decay + kv_t
        st_ref[...] = st
        o = lax.dot_general(q_ref[rows, :], st.astype(BF16), (((1,), (1,)), ((), ())),
                            preferred_element_type=F32) * (GLA_DK ** -0.5)
        ms = jnp.mean(o * o, axis=-1, keepdims=True)
        on = o * lax.rsqrt(ms + EPS) * gain
        r = r_ref[rows, :].astype(F32)
        o_ref[rows, :] = (on * (r * jax.nn.sigmoid(r))).astype(BF16)


def _gla(l, p, a_low, wgu, bg, gn, tri):
    t = GLA_T
    nt = SEQ // t

    def row(b, h, i):
        return b * nt + i

    return pl.pallas_call(
        _gla_kernel,
        out_shape=jax.ShapeDtypeStruct((ROWS, GLA_V), BF16),
        grid=(BATCH, GLA_HEADS, nt),
        in_specs=[
            pl.BlockSpec((t, GLA_DK), lambda b, h, i: (row(b, h, i), COL_QA // GLA_DK + h)),
            pl.BlockSpec((t, GLA_DK), lambda b, h, i: (row(b, h, i), COL_KA // GLA_DK + h)),
            pl.BlockSpec((t, GLA_DV), lambda b, h, i: (row(b, h, i), COL_VA // GLA_DV + h)),
            pl.BlockSpec((t, GLA_DV), lambda b, h, i: (row(b, h, i), COL_RA // GLA_DV + h)),
            pl.BlockSpec((t, RANK_PAD), lambda b, h, i: (row(b, h, i), 0)),
            pl.BlockSpec((None, RANK_PAD, GLA_DK), lambda b, h, i: (l, 0, h)),
            pl.BlockSpec((None, 1, GLA_DK), lambda b, h, i: (l, 0, h)),
            pl.BlockSpec((None, None, 1, GLA_DV), lambda b, h, i: (l, h, 0, 0)),
            pl.BlockSpec((CHUNK, CHUNK), lambda b, h, i: (0, 0)),
        ],
        out_specs=pl.BlockSpec((t, GLA_DV), lambda b, h, i: (row(b, h, i), h)),
        scratch_shapes=[pltpu.VMEM((GLA_DV, GLA_DK), F32)],
        compiler_params=_cparams(("arbitrary", "arbitrary", "arbitrary")),
        name="gla",
    )(p, p, p, p, a_low, wgu, bg, gn, tri)


SB_TQ = 256
SB_TK = 256
SB_HB = 8
SB_HW = SB_HB * SB_DH
SB_MASKED = -1e30


def _sb_kernel(q_ref, k_ref, v_ref, tt_ref, o_ref, acc_ref, carry_ref, z_ref, sp_ref, lb_ref):
    qi = pl.program_id(2)
    tt = tt_ref[...]
    scale = SB_DH ** -0.5 * LOG2E
    acc_ref[...] = jnp.zeros_like(acc_ref)
    carry_ref[...] = jnp.zeros_like(carry_ref)

    heads = [slice(h * SB_DH, (h + 1) * SB_DH) for h in range(SB_HB)]

    def keys(kb):
        return pl.ds(pl.multiple_of(kb * SB_TK, SB_TK), SB_TK)

    def step(w_kb=None, w_slot=None, s_slot=None, past=None, z_kb=None, z_slot=None):
        if w_kb is not None:
            cs = [jnp.dot(sp_ref[w_slot, h], tt, preferred_element_type=F32) for h in range(SB_HB)]
        if z_kb is not None:
            z_new = [lax.dot_general(q_ref[:, c], k_ref[keys(z_kb), c], (((1,), (1,)), ((), ())),
                                     preferred_element_type=F32) * scale for c in heads]
        if s_slot is not None:
            for h in range(SB_HB):
                z = z_ref[s_slot, h]
                sp = jnp.maximum(z, 0.0) + jnp.log2(1.0 + jnp.exp2(_neg_abs(z)))
                log_beta = z - sp
                if past is not None:
                    sp = jnp.where(past, sp, 0.0)
                    log_beta = jnp.where(past, log_beta, SB_MASKED)
                sp_ref[s_slot, h] = sp.astype(BF16)
                lb_ref[s_slot, h] = log_beta
        if w_kb is not None:
            for h, c in enumerate(heads):
                carry = carry_ref[h]
                w = jnp.exp2(lb_ref[w_slot, h] - (cs[h][:, :SB_TK] + carry)).astype(BF16)
                carry_ref[h] = carry + cs[h][:, SB_TK:]
                acc_ref[h] += jnp.dot(w, v_ref[keys(w_kb), c], preferred_element_type=F32)
        if z_kb is not None:
            for h in range(SB_HB):
                z_ref[z_slot, h] = z_new[h]

    def block(j):
        return jnp.maximum(qi - j, 0)

    t_idx = lax.broadcasted_iota(jnp.int32, (SB_TQ, SB_TK), 0)
    s_idx = lax.broadcasted_iota(jnp.int32, (SB_TQ, SB_TK), 1)
    step(z_kb=qi, z_slot=0)
    step(s_slot=0, past=s_idx < t_idx, z_kb=block(1), z_slot=1)

    def pair(i, _):
        j = 2 * i
        step(w_kb=qi - j, w_slot=0, s_slot=1, z_kb=block(j + 2), z_slot=0)
        step(w_kb=qi - j - 1, w_slot=1, s_slot=0, z_kb=block(j + 3), z_slot=1)
        return 0

    lax.fori_loop(0, qi // 2, pair, 0)

    @pl.when(qi % 2 == 1)
    def _():
        step(w_kb=1, w_slot=0, s_slot=1)
        step(w_kb=0, w_slot=1)

    @pl.when(qi % 2 == 0)
    def _():
        step(w_kb=0, w_slot=0)

    for h in range(SB_HB):
        o_ref[:, h * SB_DH:(h + 1) * SB_DH] = acc_ref[h].astype(BF16)


def _sb(p, tt):
    nq = SEQ // SB_TQ
    return pl.pallas_call(
        _sb_kernel,
        out_shape=jax.ShapeDtypeStruct((ROWS, SB_W), BF16),
        grid=(BATCH, SB_HEADS // SB_HB, nq),
        in_specs=[
            pl.BlockSpec((SB_TQ, SB_HW), lambda b, h, i: (b * nq + i, COL_QB // SB_HW + h)),
            pl.BlockSpec((SEQ, SB_HW), lambda b, h, i: (b, COL_KB // SB_HW + h)),
            pl.BlockSpec((SEQ, SB_HW), lambda b, h, i: (b, COL_VB // SB_HW + h)),
            pl.BlockSpec((SB_TK, 2 * SB_TK), lambda b, h, i: (0, 0)),
        ],
        out_specs=pl.BlockSpec((SB_TQ, SB_HW), lambda b, h, i: (b * nq + i, h)),
        scratch_shapes=[pltpu.VMEM((SB_HB, SB_TQ, SB_DH), F32),
                        pltpu.VMEM((SB_HB, SB_TQ, SB_TK), F32),
                        pltpu.VMEM((2, SB_HB, SB_TQ, SB_TK), F32),
                        pltpu.VMEM((2, SB_HB, SB_TQ, SB_TK), BF16),
                        pltpu.VMEM((2, SB_HB, SB_TQ, SB_TK), F32)],
        compiler_params=_cparams(("arbitrary", "arbitrary", "arbitrary")),
        name="sb",
    )(p, p, p, tt)


MIX_TM = 1024
MIX_TN = 512


def _mix_kernel(oa_ref, ob_ref, wa_ref, wb_ref, ga_ref, gb_ref, o_ref):
    ya = jnp.dot(oa_ref[...], wa_ref[...].astype(BF16), preferred_element_type=F32)
    yb = jnp.dot(ob_ref[...], wb_ref[...].astype(BF16), preferred_element_type=F32)
    ga = jax.nn.sigmoid(ga_ref[...].astype(F32))
    gb = jax.nn.sigmoid(gb_ref[...].astype(F32))
    o_ref[...] = (ga * ya + gb * yb).astype(BF16)


def _mix(l, o_gla, o_sb, w_gla_o, w_sb_o, p):
    tm, tn = MIX_TM, MIX_TN
    return pl.pallas_call(
        _mix_kernel,
        out_shape=jax.ShapeDtypeStruct((ROWS, D_MODEL), BF16),
        grid=(ROWS // tm, D_MODEL // tn),
        in_specs=[
            pl.BlockSpec((tm, GLA_V), lambda i, j: (i, 0)),
            pl.BlockSpec((tm, SB_W), lambda i, j: (i, 0)),
            pl.BlockSpec((None, GLA_V, tn), lambda i, j: (l, 0, j)),
            pl.BlockSpec((None, SB_W, tn), lambda i, j: (l, 0, j)),
            pl.BlockSpec((tm, tn), lambda i, j: (i, COL_GA // tn + j)),
            pl.BlockSpec((tm, tn), lambda i, j: (i, COL_GB // tn + j)),
        ],
        out_specs=pl.BlockSpec((tm, tn), lambda i, j: (i, j)),
        compiler_params=_cparams(("arbitrary", "arbitrary")),
        name="mix",
    )(o_gla, o_sb, w_gla_o, w_sb_o, p, p)


OUT_TM = 512


def _out_kernel(m_ref, w_ref, x_ref, mod_ref, ng_ref, o_ref):
    y = jnp.dot(m_ref[...], w_ref[...], preferred_element_type=F32)
    ms = jnp.mean(y * y, axis=-1, keepdims=True)
    yn = y * lax.rsqrt(ms + EPS) * ng_ref[1:2, :]
    o_ref[...] = x_ref[...] + mod_ref[2:3, :] * yn


def _out(l, mixed, w_out, x2, mod, norm_gains):
    tm = OUT_TM
    return pl.pallas_call(
        _out_kernel,
        out_shape=jax.ShapeDtypeStruct((ROWS, D_MODEL), F32),
        grid=(ROWS // tm,),
        in_specs=[
            pl.BlockSpec((tm, D_MODEL), lambda i: (i, 0)),
            pl.BlockSpec((None, D_MODEL, D_MODEL), lambda i: (l, 0, 0)),
            pl.BlockSpec((tm, D_MODEL), lambda i: (i, 0)),
            _mod_spec(l, tm),
            _gain_spec(l),
        ],
        out_specs=pl.BlockSpec((tm, D_MODEL), lambda i: (i, 0)),
        compiler_params=_cparams(("arbitrary",)),
        name="out",
    )(mixed, w_out, x2, mod, norm_gains)


FFN_TM = 512
FFN_TF = 1024


def _ffn_kernel(x_ref, mod_ref, ng_ref, w1_ref, w2_ref, o_ref, h_ref, acc_ref):
    f = pl.program_id(1)

    @pl.when(f == 0)
    def _():
        h = _modulated_norm(x_ref[...], ng_ref[2:3, :], mod_ref[3:4, :], mod_ref[4:5, :])
        h_ref[...] = h.astype(BF16)
        acc_ref[...] = jnp.zeros_like(acc_ref)

    a = jnp.dot(h_ref[...], w1_ref[...], preferred_element_type=F32)
    a = jnp.square(jnp.maximum(a, 0.0)).astype(BF16)
    acc_ref[...] += jnp.dot(a, w2_ref[...], preferred_element_type=F32)

    @pl.when(f == pl.num_programs(1) - 1)
    def _():
        y = acc_ref[...]
        ms = jnp.mean(y * y, axis=-1, keepdims=True)
        yn = y * lax.rsqrt(ms + EPS) * ng_ref[3:4, :]
        o_ref[...] = x_ref[...] + mod_ref[5:6, :] * yn


def _ffn(l, x2, mod, norm_gains, w1, w2):
    tm, tf = FFN_TM, FFN_TF
    return pl.pallas_call(
        _ffn_kernel,
        out_shape=jax.ShapeDtypeStruct((ROWS, D_MODEL), F32),
        grid=(ROWS // tm, D_FF // tf),
        in_specs=[
            pl.BlockSpec((tm, D_MODEL), lambda i, f: (i, 0)),
            _mod_spec(l, tm),
            _gain_spec(l),
            pl.BlockSpec((None, D_MODEL, tf), lambda i, f: (l, 0, f)),
            pl.BlockSpec((None, tf, D_MODEL), lambda i, f: (l, f, 0)),
        ],
        out_specs=pl.BlockSpec((tm, D_MODEL), lambda i, f: (i, 0)),
        scratch_shapes=[pltpu.VMEM((tm, D_MODEL), BF16), pltpu.VMEM((tm, D_MODEL), F32)],
        compiler_params=_cparams(("arbitrary", "arbitrary")),
        name="ffn",
    )(x2, mod, norm_gains, w1, w2)


def _cumsum_constants():
    j = np.arange(CHUNK)
    tri = (j[:, None] >= j[None, :]).astype(np.float32)
    jj = np.arange(SB_TK)
    s = np.arange(2 * SB_TK)
    tt = ((jj[:, None] > s[None, :]) | (s[None, :] >= SB_TK)).astype(np.float32)
    return jnp.asarray(tri, BF16), jnp.asarray(tt, BF16)


def kernel(x, c, w_ada, b_ada, norm_gains, w_in, w_gate_up, b_gate, gla_norm_gain,
           w_gla_o, w_sb_o, w_out, w_ff1, w_ff2):
    tri, tt = _cumsum_constants()

    c_pad = jnp.pad(c, ((0, 8 - BATCH), (0, 0)))
    mod = _ada(c_pad, w_ada, b_ada.reshape(DEPTH, 1, 6 * D_MODEL))
    mod = mod[:, :BATCH, :].reshape(DEPTH, BATCH, 6, D_MODEL)

    w_main = _pack_w_in(w_in)
    w_alow = jnp.pad(w_in[:, :, A_LOW_START:A_LOW_END],
                     ((0, 0), (0, 0), (0, RANK_PAD - GLA_GATE_RANK))).astype(BF16)
    wgu = jnp.pad(w_gate_up, ((0, 0), (0, RANK_PAD - GLA_GATE_RANK), (0, 0))).astype(BF16)
    bg = b_gate.reshape(DEPTH, 1, GLA_K)
    gn = gla_norm_gain.reshape(DEPTH, GLA_HEADS, 1, GLA_DV)
    w_out_b = w_out.astype(BF16)
    w_ff1_b = w_ff1.astype(BF16)
    w_ff2_b = w_ff2.astype(BF16)

    x2 = x.reshape(ROWS, D_MODEL)
    for l in range(DEPTH):
        p, a_low = _proj(l, x2, mod, norm_gains, w_main, w_alow)
        o_gla = _gla(l, p, a_low, wgu, bg, gn, tri)
        o_sb = _sb(p, tt)
        mixed = _mix(l, o_gla, o_sb, w_gla_o, w_sb_o, p)
        x2 = _out(l, mixed, w_out_b, x2, mod, norm_gains)
        x2 = _ffn(l, x2, mod, norm_gains, w_ff1_b, w_ff2_b)
    return x2.reshape(BATCH, SEQ, D_MODEL)
```

```python
import functools

import numpy as np
import jax
import jax.numpy as jnp
from jax import lax
from jax.experimental import pallas as pl
from jax.experimental.pallas import tpu as pltpu

F32 = jnp.float32
BF16 = jnp.bfloat16

D_MODEL = 2048
BATCH = 4
SEQ = 2048
DEPTH = 4
ROWS = BATCH * SEQ

CHUNK = 64
GLA_HEADS = 4
GLA_DK = 256
GLA_DV = 512
GLA_K = GLA_HEADS * GLA_DK
GLA_V = GLA_HEADS * GLA_DV
GLA_GATE_RANK = 16
GLA_GATE_TAU = 16.0
SB_HEADS = 16
SB_DH = 128
SB_W = SB_HEADS * SB_DH
D_FF = 4 * D_MODEL
EPS = 1e-6

LOG2E = 1.4426950408889634
LANES = 128
RANK_PAD = LANES
A_LOW_START = 2 * GLA_K + 2 * GLA_V
A_LOW_END = A_LOW_START + GLA_GATE_RANK

P_COLS = 2 * GLA_K + 2 * GLA_V + 3 * SB_W + 2 * D_MODEL
COL_QA = 0
COL_KA = GLA_K
COL_VA = 2 * GLA_K
COL_RA = 2 * GLA_K + GLA_V
COL_QB = 2 * GLA_K + 2 * GLA_V
COL_KB = COL_QB + SB_W
COL_VB = COL_KB + SB_W
COL_GA = COL_VB + SB_W
COL_GB = COL_GA + D_MODEL

VMEM_LIMIT = 56 * 1024 * 1024


def _cparams(sem):
    return pltpu.CompilerParams(dimension_semantics=sem, vmem_limit_bytes=VMEM_LIMIT)


def _softplus(z):
    return jnp.maximum(z, 0.0) + jnp.log(1.0 + jnp.exp(-jnp.abs(z)))


def _neg_abs(x):
    bits = pltpu.bitcast(x, jnp.uint32) | jnp.uint32(0x80000000)
    return pltpu.bitcast(bits, F32)


def _split_bf16(a):
    hi = a.astype(BF16)
    lo = (a - hi.astype(F32)).astype(BF16)
    return hi, lo


ADA_TK = 256


def _ada_kernel(c_ref, w_ref, b_ref, o_ref):
    @pl.when(pl.program_id(1) == 0)
    def _():
        o_ref[...] = jnp.broadcast_to(b_ref[...], o_ref.shape)

    c = c_ref[...]
    ca = (c * jax.nn.sigmoid(c)).astype(BF16)
    o_ref[...] += jnp.dot(ca, w_ref[...].astype(BF16), preferred_element_type=F32)


def _ada(c_pad, w_ada, b_ada3):
    n = 6 * D_MODEL
    return pl.pallas_call(
        _ada_kernel,
        out_shape=jax.ShapeDtypeStruct((DEPTH, 8, n), F32),
        grid=(DEPTH, D_MODEL // ADA_TK),
        in_specs=[
            pl.BlockSpec((8, ADA_TK), lambda l, k: (0, k)),
            pl.BlockSpec((None, ADA_TK, n), lambda l, k: (l, k, 0)),
            pl.BlockSpec((None, 1, n), lambda l, k: (l, 0, 0)),
        ],
        out_specs=pl.BlockSpec((None, 8, n), lambda l, k: (l, 0, 0)),
        compiler_params=_cparams(("arbitrary", "arbitrary")),
        name="ada",
    )(c_pad, w_ada, b_ada3)


PROJ_TM = 1024
PROJ_TN = 2048
PACK_TN = 1024
PACK_LO_TILES = A_LOW_START // PACK_TN


def _modulated_norm(x, gain, shift, scale):
    ms = jnp.mean(x * x, axis=-1, keepdims=True)
    y = x * lax.rsqrt(ms + EPS) * gain
    return y * (1.0 + scale) + shift


def _pack_kernel(w_ref, nxt_ref, o_ref):
    j = pl.program_id(1)
    keep = PACK_TN - GLA_GATE_RANK

    @pl.when(j < PACK_LO_TILES)
    def _():
        o_ref[...] = w_ref[...].astype(BF16)

    @pl.when(j >= PACK_LO_TILES)
    def _():
        o_ref[0:keep, :] = w_ref[GLA_GATE_RANK:PACK_TN, :].astype(BF16)
        o_ref[keep:PACK_TN, :] = nxt_ref[...].astype(BF16)


def _pack_w_in(w_in_t):
    tn = PACK_TN
    return pl.pallas_call(
        _pack_kernel,
        out_shape=jax.ShapeDtypeStruct((DEPTH, P_COLS, D_MODEL), BF16),
        grid=(DEPTH, P_COLS // tn),
        in_specs=[
            pl.BlockSpec((None, tn, D_MODEL), lambda l, j: (l, j, 0)),
            pl.BlockSpec((None, GLA_GATE_RANK, D_MODEL),
                         lambda l, j: (l, (j + 1) * (tn // GLA_GATE_RANK), 0)),
        ],
        out_specs=pl.BlockSpec((None, tn, D_MODEL), lambda l, j: (l, j, 0)),
        compiler_params=_cparams(("arbitrary", "arbitrary")),
        name="pack_w_in",
    )(w_in_t, w_in_t)


_NT = (((1,), (1,)), ((), ()))


def _proj_kernel(x_ref, mod_ref, ng_ref, wt_ref, wat_ref, p_ref, al_ref, h_ref):
    @pl.when(pl.program_id(1) == 0)
    def _():
        h = _modulated_norm(x_ref[...], ng_ref[0:1, :], mod_ref[0:1, :], mod_ref[1:2, :])
        hb = h.astype(BF16)
        h_ref[...] = hb
        al_ref[...] = lax.dot_general(hb, wat_ref[...], _NT, preferred_element_type=F32)

    p_ref[...] = lax.dot_general(h_ref[...], wt_ref[...], _NT,
                                 preferred_element_type=F32).astype(BF16)


def _mod_spec(l, tm):
    return pl.BlockSpec((None, None, 6, D_MODEL), lambda i, *_: (l, i * tm // SEQ, 0, 0))


def _gain_spec(l):
    return pl.BlockSpec((None, 4, D_MODEL), lambda *_: (l, 0, 0))


def _proj(l, x2, mod, norm_gains, w_main, w_alow):
    tm, tn = PROJ_TM, PROJ_TN
    return pl.pallas_call(
        _proj_kernel,
        out_shape=(jax.ShapeDtypeStruct((ROWS, P_COLS), BF16),
                   jax.ShapeDtypeStruct((ROWS, RANK_PAD), F32)),
        grid=(ROWS // tm, P_COLS // tn),
        in_specs=[
            pl.BlockSpec((tm, D_MODEL), lambda i, j: (i, 0)),
            _mod_spec(l, tm),
            _gain_spec(l),
            pl.BlockSpec((None, tn, D_MODEL), lambda i, j: (l, j, 0)),
            pl.BlockSpec((None, RANK_PAD, D_MODEL), lambda i, j: (l, 0, 0)),
        ],
        out_specs=(pl.BlockSpec((tm, tn), lambda i, j: (i, j)),
                   pl.BlockSpec((tm, RANK_PAD), lambda i, j: (i, 0))),
        scratch_shapes=[pltpu.VMEM((tm, D_MODEL), BF16)],
        compiler_params=_cparams(("arbitrary", "arbitrary")),
        name="proj",
    )(x2, mod, norm_gains, w_main, w_alow)


GLA_T = 512


def _gla_kernel(q_ref, k_ref, v_ref, r_ref, al_ref, wgu_ref, bg_ref, gn_ref, tri_ref,
                o_ref, st_ref):
    @pl.when(pl.program_id(2) == 0)
    def _():
        st_ref[...] = jnp.zeros_like(st_ref)

    a = jnp.dot(al_ref[...].astype(BF16), wgu_ref[...], preferred_element_type=F32) + bg_ref[...]
    log_a = -_softplus(-a) * (1.0 / GLA_GATE_TAU)
    tri = tri_ref[...]
    gain = gn_ref[...]

    n_chunks = GLA_T // CHUNK

    def chunk_rows(j):
        return slice(j * CHUNK, (j + 1) * CHUNK)

    def chunk_kv(j):
        rows = chunk_rows(j)
        hi, lo = _split_bf16(log_a[rows])
        g = (jnp.dot(tri, hi, preferred_element_type=F32)
             + jnp.dot(tri, lo, preferred_element_type=F32))
        g_tot = g[CHUNK - 1:CHUNK, :]
        kd = (k_ref[rows, :].astype(F32) * jnp.exp(g_tot - g)).astype(BF16)
        kv_t = lax.dot_general(v_ref[rows, :], kd, (((0,), (0,)), ((), ())),
                               preferred_element_type=F32)
        return jnp.exp(g_tot), kv_t

    ahead = 2
    pending = [chunk_kv(j) for j in range(ahead)]
    for j in range(n_chunks):
        decay, kv_t = pending.pop(0)
        if j + ahead < n_chunks:
            pending.append(chunk_kv(j + ahead))
        rows = chunk_rows(j)
        st = st_ref[...] * decay + kv_t
        st_ref[...] = st
        o = lax.dot_general(q_ref[rows, :], st.astype(BF16), (((1,), (1,)), ((), ())),
                            preferred_element_type=F32) * (GLA_DK ** -0.5)
        ms = jnp.mean(o * o, axis=-1, keepdims=True)
        on = o * lax.rsqrt(ms + EPS) * gain
        r = r_ref[rows, :].astype(F32)
        o_ref[rows, :] = (on * (r * jax.nn.sigmoid(r))).astype(BF16)


def _gla(l, p, a_low, wgu, bg, gn, tri):
    t = GLA_T
    nt = SEQ // t

    def row(b, h, i):
        return b * nt + i

    return pl.pallas_call(
        _gla_kernel,
        out_shape=jax.ShapeDtypeStruct((ROWS, GLA_V), BF16),
        grid=(BATCH, GLA_HEADS, nt),
        in_specs=[
            pl.BlockSpec((t, GLA_DK), lambda b, h, i: (row(b, h, i), COL_QA // GLA_DK + h)),
            pl.BlockSpec((t, GLA_DK), lambda b, h, i: (row(b, h, i), COL_KA // GLA_DK + h)),
            pl.BlockSpec((t, GLA_DV), lambda b, h, i: (row(b, h, i), COL_VA // GLA_DV + h)),
            pl.BlockSpec((t, GLA_DV), lambda b, h, i: (row(b, h, i), COL_RA // GLA_DV + h)),
            pl.BlockSpec((t, RANK_PAD), lambda b, h, i: (row(b, h, i), 0)),
            pl.BlockSpec((None, RANK_PAD, GLA_DK), lambda b, h, i: (l, 0, h)),
            pl.BlockSpec((None, 1, GLA_DK), lambda b, h, i: (l, 0, h)),
            pl.BlockSpec((None, None, 1, GLA_DV), lambda b, h, i: (l, h, 0, 0)),
            pl.BlockSpec((CHUNK, CHUNK), lambda b, h, i: (0, 0)),
        ],
        out_specs=pl.BlockSpec((t, GLA_DV), lambda b, h, i: (row(b, h, i), h)),
        scratch_shapes=[pltpu.VMEM((GLA_DV, GLA_DK), F32)],
        compiler_params=_cparams(("arbitrary", "arbitrary", "arbitrary")),
        name="gla",
    )(p, p, p, p, a_low, wgu, bg, gn, tri)


SB_TQ = 256
SB_TK = 256
SB_HB = 8
SB_HW = SB_HB * SB_DH
SB_MASKED = -1e30


def _sb_kernel(q_ref, k_ref, v_ref, tt_ref, o_ref, acc_ref, carry_ref, z_ref, sp_ref, lb_ref):
    qi = pl.program_id(2)
    tt = tt_ref[...]
    scale = SB_DH ** -0.5 * LOG2E
    acc_ref[...] = jnp.zeros_like(acc_ref)
    carry_ref[...] = jnp.zeros_like(carry_ref)

    heads = [slice(h * SB_DH, (h + 1) * SB_DH) for h in range(SB_HB)]

    def keys(kb):
        return pl.ds(pl.multiple_of(kb * SB_TK, SB_TK), SB_TK)

    def step(w_kb=None, w_slot=None, s_slot=None, past=None, z_kb=None, z_slot=None):
        if w_kb is not None:
            cs = [jnp.dot(sp_ref[w_slot, h], tt, preferred_element_type=F32) for h in range(SB_HB)]
        if z_kb is not None:
            z_new = [lax.dot_general(q_ref[:, c], k_ref[keys(z_kb), c], (((1,), (1,)), ((), ())),
                                     preferred_element_type=F32) * scale for c in heads]
        if s_slot is not None:
            for h in range(SB_HB):
                z = z_ref[s_slot, h]
                sp = jnp.maximum(z, 0.0) + jnp.log2(1.0 + jnp.exp2(_neg_abs(z)))
                log_beta = z - sp
                if past is not None:
                    sp = jnp.where(past, sp, 0.0)
                    log_beta = jnp.where(past, log_beta, SB_MASKED)
                sp_ref[s_slot, h] = sp.astype(BF16)
                lb_ref[s_slot, h] = log_beta
        if w_kb is not None:
            for h, c in enumerate(heads):
                carry = carry_ref[h]
                w = jnp.exp2(lb_ref[w_slot, h] - (cs[h][:, :SB_TK] + carry)).astype(BF16)
                carry_ref[h] = carry + cs[h][:, SB_TK:]
                acc_ref[h] += jnp.dot(w, v_ref[keys(w_kb), c], preferred_element_type=F32)
        if z_kb is not None:
            for h in range(SB_HB):
                z_ref[z_slot, h] = z_new[h]

    def block(j):
        return jnp.maximum(qi - j, 0)

    t_idx = lax.broadcasted_iota(jnp.int32, (SB_TQ, SB_TK), 0)
    s_idx = lax.broadcasted_iota(jnp.int32, (SB_TQ, SB_TK), 1)
    step(z_kb=qi, z_slot=0)
    step(s_slot=0, past=s_idx < t_idx, z_kb=block(1), z_slot=1)

    def pair(i, _):
        j = 2 * i
        step(w_kb=qi - j, w_slot=0, s_slot=1, z_kb=block(j + 2), z_slot=0)
        step(w_kb=qi - j - 1, w_slot=1, s_slot=0, z_kb=block(j + 3), z_slot=1)
        return 0

    lax.fori_loop(0, qi // 2, pair, 0)

    @pl.when(qi % 2 == 1)
    def _():
        step(w_kb=1, w_slot=0, s_slot=1)
        step(w_kb=0, w_slot=1)

    @pl.when(qi % 2 == 0)
    def _():
        step(w_kb=0, w_slot=0)

    for h in range(SB_HB):
        o_ref[:, h * SB_DH:(h + 1) * SB_DH] = acc_ref[h].astype(BF16)


def _sb(p, tt):
    nq = SEQ // SB_TQ
    return pl.pallas_call(
        _sb_kernel,
        out_shape=jax.ShapeDtypeStruct((ROWS, SB_W), BF16),
        grid=(BATCH, SB_HEADS // SB_HB, nq),
        in_specs=[
            pl.BlockSpec((SB_TQ, SB_HW), lambda b, h, i: (b * nq + i, COL_QB // SB_HW + h)),
            pl.BlockSpec((SEQ, SB_HW), lambda b, h, i: (b, COL_KB // SB_HW + h)),
            pl.BlockSpec((SEQ, SB_HW), lambda b, h, i: (b, COL_VB // SB_HW + h)),
            pl.BlockSpec((SB_TK, 2 * SB_TK), lambda b, h, i: (0, 0)),
        ],
        out_specs=pl.BlockSpec((SB_TQ, SB_HW), lambda b, h, i: (b * nq + i, h)),
        scratch_shapes=[pltpu.VMEM((SB_HB, SB_TQ, SB_DH), F32),
                        pltpu.VMEM((SB_HB, SB_TQ, SB_TK), F32),
                        pltpu.VMEM((2, SB_HB, SB_TQ, SB_TK), F32),
                        pltpu.VMEM((2, SB_HB, SB_TQ, SB_TK), BF16),
                        pltpu.VMEM((2, SB_HB, SB_TQ, SB_TK), F32)],
        compiler_params=_cparams(("arbitrary", "arbitrary", "arbitrary")),
        name="sb",
    )(p, p, p, tt)


MIX_TM = 1024
MIX_TN = 512


def _mix_kernel(oa_ref, ob_ref, wa_ref, wb_ref, ga_ref, gb_ref, o_ref):
    ya = jnp.dot(oa_ref[...], wa_ref[...].astype(BF16), preferred_element_type=F32)
    yb = jnp.dot(ob_ref[...], wb_ref[...].astype(BF16), preferred_element_type=F32)
    ga = jax.nn.sigmoid(ga_ref[...].astype(F32))
    gb = jax.nn.sigmoid(gb_ref[...].astype(F32))
    o_ref[...] = (ga * ya + gb * yb).astype(BF16)


def _mix(l, o_gla, o_sb, w_gla_o, w_sb_o, p):
    tm, tn = MIX_TM, MIX_TN
    return pl.pallas_call(
        _mix_kernel,
        out_shape=jax.ShapeDtypeStruct((ROWS, D_MODEL), BF16),
        grid=(ROWS // tm, D_MODEL // tn),
        in_specs=[
            pl.BlockSpec((tm, GLA_V), lambda i, j: (i, 0)),
            pl.BlockSpec((tm, SB_W), lambda i, j: (i, 0)),
            pl.BlockSpec((None, GLA_V, tn), lambda i, j: (l, 0, j)),
            pl.BlockSpec((None, SB_W, tn), lambda i, j: (l, 0, j)),
            pl.BlockSpec((tm, tn), lambda i, j: (i, COL_GA // tn + j)),
            pl.BlockSpec((tm, tn), lambda i, j: (i, COL_GB // tn + j)),
        ],
        out_specs=pl.BlockSpec((tm, tn), lambda i, j: (i, j)),
        compiler_params=_cparams(("arbitrary", "arbitrary")),
        name="mix",
    )(o_gla, o_sb, w_gla_o, w_sb_o, p, p)


OUT_TM = 512


def _out_kernel(m_ref, w_ref, x_ref, mod_ref, ng_ref, o_ref):
    y = jnp.dot(m_ref[...], w_ref[...], preferred_element_type=F32)
    ms = jnp.mean(y * y, axis=-1, keepdims=True)
    yn = y * lax.rsqrt(ms + EPS) * ng_ref[1:2, :]
    o_ref[...] = x_ref[...] + mod_ref[2:3, :] * yn


def _out(l, mixed, w_out, x2, mod, norm_gains):
    tm = OUT_TM
    return pl.pallas_call(
        _out_kernel,
        out_shape=jax.ShapeDtypeStruct((ROWS, D_MODEL), F32),
        grid=(ROWS // tm,),
        in_specs=[
            pl.BlockSpec((tm, D_MODEL), lambda i: (i, 0)),
            pl.BlockSpec((None, D_MODEL, D_MODEL), lambda i: (l, 0, 0)),
            pl.BlockSpec((tm, D_MODEL), lambda i: (i, 0)),
            _mod_spec(l, tm),
            _gain_spec(l),
        ],
        out_specs=pl.BlockSpec((tm, D_MODEL), lambda i: (i, 0)),
        compiler_params=_cparams(("arbitrary",)),
        name="out",
    )(mixed, w_out, x2, mod, norm_gains)


FFN_TM = 512
FFN_TF = 1024


def _ffn_kernel(x_ref, mod_ref, ng_ref, w1_ref, w2_ref, o_ref, h_ref, acc_ref):
    f = pl.program_id(1)

    @pl.when(f == 0)
    def _():
        h = _modulated_norm(x_ref[...], ng_ref[2:3, :], mod_ref[3:4, :], mod_ref[4:5, :])
        h_ref[...] = h.astype(BF16)
        acc_ref[...] = jnp.zeros_like(acc_ref)

    a = jnp.dot(h_ref[...], w1_ref[...], preferred_element_type=F32)
    a = jnp.square(jnp.maximum(a, 0.0)).astype(BF16)
    acc_ref[...] += jnp.dot(a, w2_ref[...], preferred_element_type=F32)

    @pl.when(f == pl.num_programs(1) - 1)
    def _():
        y = acc_ref[...]
        ms = jnp.mean(y * y, axis=-1, keepdims=True)
        yn = y * lax.rsqrt(ms + EPS) * ng_ref[3:4, :]
        o_ref[...] = x_ref[...] + mod_ref[5:6, :] * yn


def _ffn(l, x2, mod, norm_gains, w1, w2):
    tm, tf = FFN_TM, FFN_TF
    return pl.pallas_call(
        _ffn_kernel,
        out_shape=jax.ShapeDtypeStruct((ROWS, D_MODEL), F32),
        grid=(ROWS // tm, D_FF // tf),
        in_specs=[
            pl.BlockSpec((tm, D_MODEL), lambda i, f: (i, 0)),
            _mod_spec(l, tm),
            _gain_spec(l),
            pl.BlockSpec((None, D_MODEL, tf), lambda i, f: (l, 0, f)),
            pl.BlockSpec((None, tf, D_MODEL), lambda i, f: (l, f, 0)),
        ],
        out_specs=pl.BlockSpec((tm, D_MODEL), lambda i, f: (i, 0)),
        scratch_shapes=[pltpu.VMEM((tm, D_MODEL), BF16), pltpu.VMEM((tm, D_MODEL), F32)],
        compiler_params=_cparams(("arbitrary", "arbitrary")),
        name="ffn",
    )(x2, mod, norm_gains, w1, w2)


def _cumsum_constants():
    j = np.arange(CHUNK)
    tri = (j[:, None] >= j[None, :]).astype(np.float32)
    jj = np.arange(SB_TK)
    s = np.arange(2 * SB_TK)
    tt = ((jj[:, None] > s[None, :]) | (s[None, :] >= SB_TK)).astype(np.float32)
    return jnp.asarray(tri, BF16), jnp.asarray(tt, BF16)


def kernel(x, c, w_ada, b_ada, norm_gains, w_in, w_gate_up, b_gate, gla_norm_gain,
           w_gla_o, w_sb_o, w_out, w_ff1, w_ff2):
    tri, tt = _cumsum_constants()

    c_pad = jnp.pad(c, ((0, 8 - BATCH), (0, 0)))
    mod = _ada(c_pad, w_ada, b_ada.reshape(DEPTH, 1, 6 * D_MODEL))
    mod = mod[:, :BATCH, :].reshape(DEPTH, BATCH, 6, D_MODEL)

    w_in_t = jnp.swapaxes(w_in, 1, 2)
    w_main = _pack_w_in(w_in_t)
    w_alow = jnp.pad(w_in_t[:, A_LOW_START:A_LOW_END, :],
                     ((0, 0), (0, RANK_PAD - GLA_GATE_RANK), (0, 0))).astype(BF16)
    wgu = jnp.pad(w_gate_up, ((0, 0), (0, RANK_PAD - GLA_GATE_RANK), (0, 0))).astype(BF16)
    bg = b_gate.reshape(DEPTH, 1, GLA_K)
    gn = gla_norm_gain.reshape(DEPTH, GLA_HEADS, 1, GLA_DV)
    w_out_b = w_out.astype(BF16)
    w_ff1_b = w_ff1.astype(BF16)
    w_ff2_b = w_ff2.astype(BF16)

    x2 = x.reshape(ROWS, D_MODEL)
    for l in range(DEPTH):
        p, a_low = _proj(l, x2, mod, norm_gains, w_main, w_alow)
        o_gla = _gla(l, p, a_low, wgu, bg, gn, tri)
        o_sb = _sb(p, tt)
        mixed = _mix(l, o_gla, o_sb, w_gla_o, w_sb_o, p)
        x2 = _out(l, mixed, w_out_b, x2, mod, norm_gains)
        x2 = _ffn(l, x2, mod, norm_gains, w_ff1_b, w_ff2_b)
    return x2.reshape(BATCH, SEQ, D_MODEL)
```

```python
import functools

import numpy as np
import jax
import jax.numpy as jnp
from jax import lax
from jax.experimental import pallas as pl
from jax.experimental.pallas import tpu as pltpu

F32 = jnp.float32
BF16 = jnp.bfloat16

D_MODEL = 2048
BATCH = 4
SEQ = 2048
DEPTH = 4
ROWS = BATCH * SEQ

CHUNK = 64
GLA_HEADS = 4
GLA_DK = 256
GLA_DV = 512
GLA_K = GLA_HEADS * GLA_DK
GLA_V = GLA_HEADS * GLA_DV
GLA_GATE_RANK = 16
GLA_GATE_TAU = 16.0
SB_HEADS = 16
SB_DH = 128
SB_W = SB_HEADS * SB_DH
D_FF = 4 * D_MODEL
EPS = 1e-6

LOG2E = 1.4426950408889634
LANES = 128
RANK_PAD = LANES
A_LOW_START = 2 * GLA_K + 2 * GLA_V
A_LOW_END = A_LOW_START + GLA_GATE_RANK

P_COLS = 2 * GLA_K + 2 * GLA_V + 3 * SB_W + 2 * D_MODEL
COL_QA = 0
COL_KA = GLA_K
COL_VA = 2 * GLA_K
COL_RA = 2 * GLA_K + GLA_V
COL_QB = 2 * GLA_K + 2 * GLA_V
COL_KB = COL_QB + SB_W
COL_VB = COL_KB + SB_W
COL_GA = COL_VB + SB_W
COL_GB = COL_GA + D_MODEL

VMEM_LIMIT = 56 * 1024 * 1024


def _cparams(sem):
    return pltpu.CompilerParams(dimension_semantics=sem, vmem_limit_bytes=VMEM_LIMIT)


def _softplus(z):
    return jnp.maximum(z, 0.0) + jnp.log(1.0 + jnp.exp(-jnp.abs(z)))


def _neg_abs(x):
    bits = pltpu.bitcast(x, jnp.uint32) | jnp.uint32(0x80000000)
    return pltpu.bitcast(bits, F32)


def _split_bf16(a):
    hi = a.astype(BF16)
    lo = (a - hi.astype(F32)).astype(BF16)
    return hi, lo


ADA_TK = 256


def _ada_kernel(c_ref, w_ref, b_ref, o_ref):
    @pl.when(pl.program_id(1) == 0)
    def _():
        o_ref[...] = jnp.broadcast_to(b_ref[...], o_ref.shape)

    c = c_ref[...]
    ca = (c * jax.nn.sigmoid(c)).astype(BF16)
    o_ref[...] += jnp.dot(ca, w_ref[...].astype(BF16), preferred_element_type=F32)


def _ada(c_pad, w_ada, b_ada3):
    n = 6 * D_MODEL
    return pl.pallas_call(
        _ada_kernel,
        out_shape=jax.ShapeDtypeStruct((DEPTH, 8, n), F32),
        grid=(DEPTH, D_MODEL // ADA_TK),
        in_specs=[
            pl.BlockSpec((8, ADA_TK), lambda l, k: (0, k)),
            pl.BlockSpec((None, ADA_TK, n), lambda l, k: (l, k, 0)),
            pl.BlockSpec((None, 1, n), lambda l, k: (l, 0, 0)),
        ],
        out_specs=pl.BlockSpec((None, 8, n), lambda l, k: (l, 0, 0)),
        compiler_params=_cparams(("arbitrary", "arbitrary")),
        name="ada",
    )(c_pad, w_ada, b_ada3)


PROJ_TM = 1024
PROJ_TN = 2048
PACK_TN = 1024
PACK_LO_TILES = A_LOW_START // PACK_TN


def _modulated_norm(x, gain, shift, scale):
    ms = jnp.mean(x * x, axis=-1, keepdims=True)
    y = x * lax.rsqrt(ms + EPS) * gain
    return y * (1.0 + scale) + shift


def _packed_rows(tile, w, nxt, rows, lo_tiles):
    shifted = jnp.concatenate([w[GLA_GATE_RANK:rows, :], nxt], axis=0)
    return jnp.where(tile < lo_tiles, w, shifted).astype(BF16)


def _pack_specs(l, rows, tile_of):
    per = rows // GLA_GATE_RANK
    return [pl.BlockSpec((None, rows, D_MODEL), lambda *g: (l, tile_of(*g), 0)),
            pl.BlockSpec((None, GLA_GATE_RANK, D_MODEL),
                         lambda *g: (l, (tile_of(*g) + 1) * per, 0))]


def _pack_kernel(w_ref, nxt_ref, o_ref):
    o_ref[...] = _packed_rows(pl.program_id(0), w_ref[...], nxt_ref[...], PACK_TN, PACK_LO_TILES)


def _pack_w_in(l, w_in_t):
    tn = PACK_TN
    return pl.pallas_call(
        _pack_kernel,
        out_shape=jax.ShapeDtypeStruct((P_COLS, D_MODEL), BF16),
        grid=(P_COLS // tn,),
        in_specs=_pack_specs(l, tn, lambda j: j),
        out_specs=pl.BlockSpec((tn, D_MODEL), lambda j: (j, 0)),
        compiler_params=_cparams(("arbitrary",)),
        name="pack_w_in",
    )(w_in_t, w_in_t)


_NT = (((1,), (1,)), ((), ()))


def _proj_kernel(x_ref, mod_ref, ng_ref, wt_ref, wat_ref, p_ref, al_ref, h_ref):
    @pl.when(pl.program_id(1) == 0)
    def _():
        h = _modulated_norm(x_ref[...], ng_ref[0:1, :], mod_ref[0:1, :], mod_ref[1:2, :])
        hb = h.astype(BF16)
        h_ref[...] = hb
        al_ref[...] = lax.dot_general(hb, wat_ref[...], _NT, preferred_element_type=F32)

    p_ref[...] = lax.dot_general(h_ref[...], wt_ref[...], _NT,
                                 preferred_element_type=F32).astype(BF16)


def _mod_spec(l, tm):
    return pl.BlockSpec((None, None, 6, D_MODEL), lambda i, *_: (l, i * tm // SEQ, 0, 0))


def _gain_spec(l):
    return pl.BlockSpec((None, 4, D_MODEL), lambda *_: (l, 0, 0))


def _proj(l, x2, mod, norm_gains, w_main, w_alow):
    tm, tn = PROJ_TM, PROJ_TN
    return pl.pallas_call(
        _proj_kernel,
        out_shape=(jax.ShapeDtypeStruct((ROWS, P_COLS), BF16),
                   jax.ShapeDtypeStruct((ROWS, RANK_PAD), F32)),
        grid=(ROWS // tm, P_COLS // tn),
        in_specs=[
            pl.BlockSpec((tm, D_MODEL), lambda i, j: (i, 0)),
            _mod_spec(l, tm),
            _gain_spec(l),
            pl.BlockSpec((tn, D_MODEL), lambda i, j: (j, 0)),
            pl.BlockSpec((None, RANK_PAD, D_MODEL), lambda i, j: (l, 0, 0)),
        ],
        out_specs=(pl.BlockSpec((tm, tn), lambda i, j: (i, j)),
                   pl.BlockSpec((tm, RANK_PAD), lambda i, j: (i, 0))),
        scratch_shapes=[pltpu.VMEM((tm, D_MODEL), BF16)],
        compiler_params=_cparams(("arbitrary", "arbitrary")),
        name="proj",
    )(x2, mod, norm_gains, w_main, w_alow)


GLA_T = 512


def _gla_kernel(q_ref, k_ref, v_ref, r_ref, al_ref, wgu_ref, bg_ref, gn_ref, tri_ref,
                o_ref, st_ref):
    @pl.when(pl.program_id(2) == 0)
    def _():
        st_ref[...] = jnp.zeros_like(st_ref)

    a = jnp.dot(al_ref[...].astype(BF16), wgu_ref[...], preferred_element_type=F32) + bg_ref[...]
    log_a = -_softplus(-a) * (1.0 / GLA_GATE_TAU)
    tri = tri_ref[...]
    gain = gn_ref[...]

    n_chunks = GLA_T // CHUNK

    def chunk_rows(j):
        return slice(j * CHUNK, (j + 1) * CHUNK)

    def chunk_kv(j):
        rows = chunk_rows(j)
        hi, lo = _split_bf16(log_a[rows])
        g = (jnp.dot(tri, hi, preferred_element_type=F32)
             + jnp.dot(tri, lo, preferred_element_type=F32))
        g_tot = g[CHUNK - 1:CHUNK, :]
        kd = (k_ref[rows, :].astype(F32) * jnp.exp(g_tot - g)).astype(BF16)
        kv_t = lax.dot_general(v_ref[rows, :], kd, (((0,), (0,)), ((), ())),
                               preferred_element_type=F32)
        return jnp.exp(g_tot), kv_t

    ahead = 2
    pending = [chunk_kv(j) for j in range(ahead)]
    for j in range(n_chunks):
        decay, kv_t = pending.pop(0)
        if j + ahead < n_chunks:
            pending.append(chunk_kv(j + ahead))
        rows = chunk_rows(j)
        st = st_ref[...] * decay + kv_t
        st_ref[...] = st
        o = lax.dot_general(q_ref[rows, :], st.astype(BF16), (((1,), (1,)), ((), ())),
                            preferred_element_type=F32) * (GLA_DK ** -0.5)
        ms = jnp.mean(o * o, axis=-1, keepdims=True)
        on = o * lax.rsqrt(ms + EPS) * gain
        r = r_ref[rows, :].astype(F32)
        o_ref[rows, :] = (on * (r * jax.nn.sigmoid(r))).astype(BF16)


def _gla(l, p, a_low, wgu, bg, gn, tri):
    t = GLA_T
    nt = SEQ // t

    def row(b, h, i):
        return b * nt + i

    return pl.pallas_call(
        _gla_kernel,
        out_shape=jax.ShapeDtypeStruct((ROWS, GLA_V), BF16),
        grid=(BATCH, GLA_HEADS, nt),
        in_specs=[
            pl.BlockSpec((t, GLA_DK), lambda b, h, i: (row(b, h, i), COL_QA // GLA_DK + h)),
            pl.BlockSpec((t, GLA_DK), lambda b, h, i: (row(b, h, i), COL_KA // GLA_DK + h)),
            pl.BlockSpec((t, GLA_DV), lambda b, h, i: (row(b, h, i), COL_VA // GLA_DV + h)),
            pl.BlockSpec((t, GLA_DV), lambda b, h, i: (row(b, h, i), COL_RA // GLA_DV + h)),
            pl.BlockSpec((t, RANK_PAD), lambda b, h, i: (row(b, h, i), 0)),
            pl.BlockSpec((None, RANK_PAD, GLA_DK), lambda b, h, i: (l, 0, h)),
            pl.BlockSpec((None, 1, GLA_DK), lambda b, h, i: (l, 0, h)),
            pl.BlockSpec((None, None, 1, GLA_DV), lambda b, h, i: (l, h, 0, 0)),
            pl.BlockSpec((CHUNK, CHUNK), lambda b, h, i: (0, 0)),
        ],
        out_specs=pl.BlockSpec((t, GLA_DV), lambda b, h, i: (row(b, h, i), h)),
        scratch_shapes=[pltpu.VMEM((GLA_DV, GLA_DK), F32)],
        compiler_params=_cparams(("arbitrary", "arbitrary", "arbitrary")),
        name="gla",
    )(p, p, p, p, a_low, wgu, bg, gn, tri)


SB_TQ = 256
SB_TK = 256
SB_HB = 8
SB_HW = SB_HB * SB_DH
SB_MASKED = -1e30


def _sb_kernel(q_ref, k_ref, v_ref, tt_ref, o_ref, acc_ref, carry_ref, z_ref, sp_ref, lb_ref):
    qi = pl.program_id(2)
    tt = tt_ref[...]
    scale = SB_DH ** -0.5 * LOG2E
    acc_ref[...] = jnp.zeros_like(acc_ref)
    carry_ref[...] = jnp.zeros_like(carry_ref)

    heads = [slice(h * SB_DH, (h + 1) * SB_DH) for h in range(SB_HB)]

    def keys(kb):
        return pl.ds(pl.multiple_of(kb * SB_TK, SB_TK), SB_TK)

    def step(w_kb=None, w_slot=None, s_slot=None, past=None, z_kb=None, z_slot=None):
        if w_kb is not None:
            cs = [jnp.dot(sp_ref[w_slot, h], tt, preferred_element_type=F32) for h in range(SB_HB)]
        if z_kb is not None:
            z_new = [lax.dot_general(q_ref[:, c], k_ref[keys(z_kb), c], (((1,), (1,)), ((), ())),
                                     preferred_element_type=F32) * scale for c in heads]
        if s_slot is not None:
            for h in range(SB_HB):
                z = z_ref[s_slot, h]
                sp = jnp.maximum(z, 0.0) + jnp.log2(1.0 + jnp.exp2(_neg_abs(z)))
                log_beta = z - sp
                if past is not None:
                    sp = jnp.where(past, sp, 0.0)
                    log_beta = jnp.where(past, log_beta, SB_MASKED)
                sp_ref[s_slot, h] = sp.astype(BF16)
                lb_ref[s_slot, h] = log_beta
        if w_kb is not None:
            for h, c in enumerate(heads):
                carry = carry_ref[h]
                w = jnp.exp2(lb_ref[w_slot, h] - (cs[h][:, :SB_TK] + carry)).astype(BF16)
                carry_ref[h] = carry + cs[h][:, SB_TK:]
                acc_ref[h] += jnp.dot(w, v_ref[keys(w_kb), c], preferred_element_type=F32)
        if z_kb is not None:
            for h in range(SB_HB):
                z_ref[z_slot, h] = z_new[h]

    def block(j):
        return jnp.maximum(qi - j, 0)

    t_idx = lax.broadcasted_iota(jnp.int32, (SB_TQ, SB_TK), 0)
    s_idx = lax.broadcasted_iota(jnp.int32, (SB_TQ, SB_TK), 1)
    step(z_kb=qi, z_slot=0)
    step(s_slot=0, past=s_idx < t_idx, z_kb=block(1), z_slot=1)

    def pair(i, _):
        j = 2 * i
        step(w_kb=qi - j, w_slot=0, s_slot=1, z_kb=block(j + 2), z_slot=0)
        step(w_kb=qi - j - 1, w_slot=1, s_slot=0, z_kb=block(j + 3), z_slot=1)
        return 0

    lax.fori_loop(0, qi // 2, pair, 0)

    @pl.when(qi % 2 == 1)
    def _():
        step(w_kb=1, w_slot=0, s_slot=1)
        step(w_kb=0, w_slot=1)

    @pl.when(qi % 2 == 0)
    def _():
        step(w_kb=0, w_slot=0)

    for h in range(SB_HB):
        o_ref[:, h * SB_DH:(h + 1) * SB_DH] = acc_ref[h].astype(BF16)


def _sb(p, tt):
    nq = SEQ // SB_TQ
    return pl.pallas_call(
        _sb_kernel,
        out_shape=jax.ShapeDtypeStruct((ROWS, SB_W), BF16),
        grid=(BATCH, SB_HEADS // SB_HB, nq),
        in_specs=[
            pl.BlockSpec((SB_TQ, SB_HW), lambda b, h, i: (b * nq + i, COL_QB // SB_HW + h)),
            pl.BlockSpec((SEQ, SB_HW), lambda b, h, i: (b, COL_KB // SB_HW + h)),
            pl.BlockSpec((SEQ, SB_HW), lambda b, h, i: (b, COL_VB // SB_HW + h)),
            pl.BlockSpec((SB_TK, 2 * SB_TK), lambda b, h, i: (0, 0)),
        ],
        out_specs=pl.BlockSpec((SB_TQ, SB_HW), lambda b, h, i: (b * nq + i, h)),
        scratch_shapes=[pltpu.VMEM((SB_HB, SB_TQ, SB_DH), F32),
                        pltpu.VMEM((SB_HB, SB_TQ, SB_TK), F32),
                        pltpu.VMEM((2, SB_HB, SB_TQ, SB_TK), F32),
                        pltpu.VMEM((2, SB_HB, SB_TQ, SB_TK), BF16),
                        pltpu.VMEM((2, SB_HB, SB_TQ, SB_TK), F32)],
        compiler_params=_cparams(("arbitrary", "arbitrary", "arbitrary")),
        name="sb",
    )(p, p, p, tt)


MIX_TM = 1024
MIX_TN = 512


def _mix_kernel(oa_ref, ob_ref, wa_ref, wb_ref, ga_ref, gb_ref, o_ref):
    ya = jnp.dot(oa_ref[...], wa_ref[...].astype(BF16), preferred_element_type=F32)
    yb = jnp.dot(ob_ref[...], wb_ref[...].astype(BF16), preferred_element_type=F32)
    ga = jax.nn.sigmoid(ga_ref[...].astype(F32))
    gb = jax.nn.sigmoid(gb_ref[...].astype(F32))
    o_ref[...] = (ga * ya + gb * yb).astype(BF16)


def _mix(l, o_gla, o_sb, w_gla_o, w_sb_o, p):
    tm, tn = MIX_TM, MIX_TN
    return pl.pallas_call(
        _mix_kernel,
        out_shape=jax.ShapeDtypeStruct((ROWS, D_MODEL), BF16),
        grid=(ROWS // tm, D_MODEL // tn),
        in_specs=[
            pl.BlockSpec((tm, GLA_V), lambda i, j: (i, 0)),
            pl.BlockSpec((tm, SB_W), lambda i, j: (i, 0)),
            pl.BlockSpec((None, GLA_V, tn), lambda i, j: (l, 0, j)),
            pl.BlockSpec((None, SB_W, tn), lambda i, j: (l, 0, j)),
            pl.BlockSpec((tm, tn), lambda i, j: (i, COL_GA // tn + j)),
            pl.BlockSpec((tm, tn), lambda i, j: (i, COL_GB // tn + j)),
        ],
        out_specs=pl.BlockSpec((tm, tn), lambda i, j: (i, j)),
        compiler_params=_cparams(("arbitrary", "arbitrary")),
        name="mix",
    )(o_gla, o_sb, w_gla_o, w_sb_o, p, p)


OUT_TM = 512


def _out_kernel(m_ref, w_ref, x_ref, mod_ref, ng_ref, o_ref):
    y = jnp.dot(m_ref[...], w_ref[...], preferred_element_type=F32)
    ms = jnp.mean(y * y, axis=-1, keepdims=True)
    yn = y * lax.rsqrt(ms + EPS) * ng_ref[1:2, :]
    o_ref[...] = x_ref[...] + mod_ref[2:3, :] * yn


def _out(l, mixed, w_out, x2, mod, norm_gains):
    tm = OUT_TM
    return pl.pallas_call(
        _out_kernel,
        out_shape=jax.ShapeDtypeStruct((ROWS, D_MODEL), F32),
        grid=(ROWS // tm,),
        in_specs=[
            pl.BlockSpec((tm, D_MODEL), lambda i: (i, 0)),
            pl.BlockSpec((D_MODEL, D_MODEL), lambda i: (0, 0)),
            pl.BlockSpec((tm, D_MODEL), lambda i: (i, 0)),
            _mod_spec(l, tm),
            _gain_spec(l),
        ],
        out_specs=pl.BlockSpec((tm, D_MODEL), lambda i: (i, 0)),
        compiler_params=_cparams(("arbitrary",)),
        name="out",
    )(mixed, w_out, x2, mod, norm_gains)


FFN_TM = 512
FFN_TF = 1024


FFN_STEPS = (ROWS // FFN_TM) * (D_FF // FFN_TF)
CAST_IN_ROWS = P_COLS // FFN_STEPS
CAST_IN_LO_TILES = A_LOW_START // CAST_IN_ROWS
CAST_OUT_ROWS = D_MODEL // FFN_STEPS
CAST_FF1_ROWS = D_MODEL // FFN_STEPS
CAST_FF2_ROWS = D_FF // FFN_STEPS


def _ffn_cast_kernel(x_ref, mod_ref, ng_ref, w1_ref, w2_ref,
                     win_ref, winx_ref, wo_ref, wf1_ref, wf2_ref,
                     o_ref, win_o, wo_o, wf1_o, wf2_o, h_ref, acc_ref):
    def cast_slabs():
        step = pl.program_id(0) * pl.num_programs(1) + pl.program_id(1)
        win_o[...] = _packed_rows(step, win_ref[...], winx_ref[...], CAST_IN_ROWS, CAST_IN_LO_TILES)
        wo_o[...] = wo_ref[...].astype(BF16)
        wf1_o[...] = wf1_ref[...].astype(BF16)
        wf2_o[...] = wf2_ref[...].astype(BF16)

    _ffn_kernel(x_ref, mod_ref, ng_ref, w1_ref, w2_ref, o_ref, h_ref, acc_ref, beside_dots=cast_slabs)


def _ffn_kernel(x_ref, mod_ref, ng_ref, w1_ref, w2_ref, o_ref, h_ref, acc_ref, beside_dots=None):
    f = pl.program_id(1)

    @pl.when(f == 0)
    def _():
        h = _modulated_norm(x_ref[...], ng_ref[2:3, :], mod_ref[3:4, :], mod_ref[4:5, :])
        h_ref[...] = h.astype(BF16)
        acc_ref[...] = jnp.zeros_like(acc_ref)

    if beside_dots is not None:
        beside_dots()
    a = jnp.dot(h_ref[...], w1_ref[...], preferred_element_type=F32)
    a = jnp.square(jnp.maximum(a, 0.0)).astype(BF16)
    acc_ref[...] += jnp.dot(a, w2_ref[...], preferred_element_type=F32)

    @pl.when(f == pl.num_programs(1) - 1)
    def _():
        y = acc_ref[...]
        ms = jnp.mean(y * y, axis=-1, keepdims=True)
        yn = y * lax.rsqrt(ms + EPS) * ng_ref[3:4, :]
        o_ref[...] = x_ref[...] + mod_ref[5:6, :] * yn


def _ffn(l, x2, mod, norm_gains, w1, w2, next_f32=None):
    tm, tf = FFN_TM, FFN_TF
    nf = D_FF // tf
    in_specs = [
        pl.BlockSpec((tm, D_MODEL), lambda i, f: (i, 0)),
        _mod_spec(l, tm),
        _gain_spec(l),
        pl.BlockSpec((D_MODEL, tf), lambda i, f: (0, f)),
        pl.BlockSpec((tf, D_MODEL), lambda i, f: (f, 0)),
    ]
    x_spec = pl.BlockSpec((tm, D_MODEL), lambda i, f: (i, 0))
    x_shape = jax.ShapeDtypeStruct((ROWS, D_MODEL), F32)
    scratch = [pltpu.VMEM((tm, D_MODEL), BF16), pltpu.VMEM((tm, D_MODEL), F32)]
    params = _cparams(("arbitrary", "arbitrary"))
    if next_f32 is None:
        return pl.pallas_call(
            _ffn_kernel, out_shape=x_shape, grid=(ROWS // tm, nf), in_specs=in_specs,
            out_specs=x_spec, scratch_shapes=scratch, compiler_params=params, name="ffn",
        )(x2, mod, norm_gains, w1, w2)

    def step(i, f):
        return i * nf + f

    def slab(rows, cols):
        return pl.BlockSpec((rows, cols), lambda i, f: (step(i, f), 0))

    def slab_of_next(rows, cols):
        return pl.BlockSpec((None, rows, cols), lambda i, f: (l + 1, step(i, f), 0))

    w_in_t, w_out, w_ff1, w_ff2 = next_f32
    return pl.pallas_call(
        _ffn_cast_kernel,
        out_shape=(x_shape,
                   jax.ShapeDtypeStruct((P_COLS, D_MODEL), BF16),
                   jax.ShapeDtypeStruct((D_MODEL, D_MODEL), BF16),
                   jax.ShapeDtypeStruct((D_MODEL, D_FF), BF16),
                   jax.ShapeDtypeStruct((D_FF, D_MODEL), BF16)),
        grid=(ROWS // tm, nf),
        in_specs=in_specs + _pack_specs(l + 1, CAST_IN_ROWS, step) + [
            slab_of_next(CAST_OUT_ROWS, D_MODEL),
            slab_of_next(CAST_FF1_ROWS, D_FF),
            slab_of_next(CAST_FF2_ROWS, D_MODEL),
        ],
        out_specs=(x_spec, slab(CAST_IN_ROWS, D_MODEL), slab(CAST_OUT_ROWS, D_MODEL),
                   slab(CAST_FF1_ROWS, D_FF), slab(CAST_FF2_ROWS, D_MODEL)),
        scratch_shapes=scratch, compiler_params=params, name="ffn",
    )(x2, mod, norm_gains, w1, w2, w_in_t, w_in_t, w_out, w_ff1, w_ff2)


def _cumsum_constants():
    j = np.arange(CHUNK)
    tri = (j[:, None] >= j[None, :]).astype(np.float32)
    jj = np.arange(SB_TK)
    s = np.arange(2 * SB_TK)
    tt = ((jj[:, None] > s[None, :]) | (s[None, :] >= SB_TK)).astype(np.float32)
    return jnp.asarray(tri, BF16), jnp.asarray(tt, BF16)


def kernel(x, c, w_ada, b_ada, norm_gains, w_in, w_gate_up, b_gate, gla_norm_gain,
           w_gla_o, w_sb_o, w_out, w_ff1, w_ff2):
    tri, tt = _cumsum_constants()

    c_pad = jnp.pad(c, ((0, 8 - BATCH), (0, 0)))
    mod = _ada(c_pad, w_ada, b_ada.reshape(DEPTH, 1, 6 * D_MODEL))
    mod = mod[:, :BATCH, :].reshape(DEPTH, BATCH, 6, D_MODEL)

    w_in_t = jnp.swapaxes(w_in, 1, 2)
    w_alow = jnp.pad(w_in_t[:, A_LOW_START:A_LOW_END, :],
                     ((0, 0), (0, RANK_PAD - GLA_GATE_RANK), (0, 0))).astype(BF16)
    wgu = jnp.pad(w_gate_up, ((0, 0), (0, RANK_PAD - GLA_GATE_RANK), (0, 0))).astype(BF16)
    bg = b_gate.reshape(DEPTH, 1, GLA_K)
    gn = gla_norm_gain.reshape(DEPTH, GLA_HEADS, 1, GLA_DV)
    w_main = _pack_w_in(0, w_in_t)
    w_out_b = w_out[0].astype(BF16)
    w_ff1_b = w_ff1[0].astype(BF16)
    w_ff2_b = w_ff2[0].astype(BF16)

    x2 = x.reshape(ROWS, D_MODEL)
    for l in range(DEPTH):
        p, a_low = _proj(l, x2, mod, norm_gains, w_main, w_alow)
        o_gla = _gla(l, p, a_low, wgu, bg, gn, tri)
        o_sb = _sb(p, tt)
        mixed = _mix(l, o_gla, o_sb, w_gla_o, w_sb_o, p)
        x2 = _out(l, mixed, w_out_b, x2, mod, norm_gains)
        if l + 1 < DEPTH:
            x2, w_main, w_out_b, w_ff1_b, w_ff2_b = _ffn(
                l, x2, mod, norm_gains, w_ff1_b, w_ff2_b, next_f32=(w_in_t, w_out, w_ff1, w_ff2))
        else:
            x2 = _ffn(l, x2, mod, norm_gains, w_ff1_b, w_ff2_b)
    return x2.reshape(BATCH, SEQ, D_MODEL)
```

```python
import functools

import numpy as np
import jax
import jax.numpy as jnp
from jax import lax
from jax.experimental import pallas as pl
from jax.experimental.pallas import tpu as pltpu

F32 = jnp.float32
BF16 = jnp.bfloat16

D_MODEL = 2048
BATCH = 4
SEQ = 2048
DEPTH = 4
ROWS = BATCH * SEQ

CHUNK = 64
GLA_HEADS = 4
GLA_DK = 256
GLA_DV = 512
GLA_K = GLA_HEADS * GLA_DK
GLA_V = GLA_HEADS * GLA_DV
GLA_GATE_RANK = 16
GLA_GATE_TAU = 16.0
SB_HEADS = 16
SB_DH = 128
SB_W = SB_HEADS * SB_DH
D_FF = 4 * D_MODEL
EPS = 1e-6

LOG2E = 1.4426950408889634
LANES = 128
RANK_PAD = LANES
A_LOW_START = 2 * GLA_K + 2 * GLA_V
A_LOW_END = A_LOW_START + GLA_GATE_RANK

P_COLS = 2 * GLA_K + 2 * GLA_V + 3 * SB_W + 2 * D_MODEL
COL_QA = 0
COL_KA = GLA_K
COL_VA = 2 * GLA_K
COL_RA = 2 * GLA_K + GLA_V
COL_QB = 2 * GLA_K + 2 * GLA_V
COL_KB = COL_QB + SB_W
COL_VB = COL_KB + SB_W
COL_GA = COL_VB + SB_W
COL_GB = COL_GA + D_MODEL

VMEM_LIMIT = 56 * 1024 * 1024


def _cparams(sem):
    return pltpu.CompilerParams(dimension_semantics=sem, vmem_limit_bytes=VMEM_LIMIT)


def _softplus(z):
    return jnp.maximum(z, 0.0) + jnp.log(1.0 + jnp.exp(-jnp.abs(z)))


def _neg_abs(x):
    bits = pltpu.bitcast(x, jnp.uint32) | jnp.uint32(0x80000000)
    return pltpu.bitcast(bits, F32)


def _split_bf16(a):
    hi = a.astype(BF16)
    lo = (a - hi.astype(F32)).astype(BF16)
    return hi, lo


ADA_TK = 256


def _ada_kernel(c_ref, w_ref, b_ref, o_ref):
    @pl.when(pl.program_id(1) == 0)
    def _():
        o_ref[...] = jnp.broadcast_to(b_ref[...], o_ref.shape)

    c = c_ref[...]
    ca = (c * jax.nn.sigmoid(c)).astype(BF16)
    o_ref[...] += jnp.dot(ca, w_ref[...].astype(BF16), preferred_element_type=F32)


def _ada(c_pad, w_ada, b_ada3):
    n = 6 * D_MODEL
    return pl.pallas_call(
        _ada_kernel,
        out_shape=jax.ShapeDtypeStruct((DEPTH, 8, n), F32),
        grid=(DEPTH, D_MODEL // ADA_TK),
        in_specs=[
            pl.BlockSpec((8, ADA_TK), lambda l, k: (0, k)),
            pl.BlockSpec((None, ADA_TK, n), lambda l, k: (l, k, 0)),
            pl.BlockSpec((None, 1, n), lambda l, k: (l, 0, 0)),
        ],
        out_specs=pl.BlockSpec((None, 8, n), lambda l, k: (l, 0, 0)),
        compiler_params=_cparams(("arbitrary", "arbitrary")),
        name="ada",
    )(c_pad, w_ada, b_ada3)


PROJ_TM = 1024
PROJ_TN = 2048
PACK_TN = 1024
PACK_LO_TILES = A_LOW_START // PACK_TN


def _modulated_norm(x, gain, shift, scale):
    ms = jnp.mean(x * x, axis=-1, keepdims=True)
    y = x * lax.rsqrt(ms + EPS) * gain
    return y * (1.0 + scale) + shift


def _packed_rows(tile, w, nxt, rows, lo_tiles):
    shifted = jnp.concatenate([w[GLA_GATE_RANK:rows, :], nxt], axis=0)
    return jnp.where(tile < lo_tiles, w, shifted).astype(BF16)


def _pack_specs(l, rows, tile_of):
    per = rows // GLA_GATE_RANK
    return [pl.BlockSpec((None, rows, D_MODEL), lambda *g: (l, tile_of(*g), 0)),
            pl.BlockSpec((None, GLA_GATE_RANK, D_MODEL),
                         lambda *g: (l, (tile_of(*g) + 1) * per, 0))]


def _pack_kernel(w_ref, nxt_ref, o_ref):
    o_ref[...] = _packed_rows(pl.program_id(0), w_ref[...], nxt_ref[...], PACK_TN, PACK_LO_TILES)


def _pack_w_in(l, w_in_t):
    tn = PACK_TN
    return pl.pallas_call(
        _pack_kernel,
        out_shape=jax.ShapeDtypeStruct((P_COLS, D_MODEL), BF16),
        grid=(P_COLS // tn,),
        in_specs=_pack_specs(l, tn, lambda j: j),
        out_specs=pl.BlockSpec((tn, D_MODEL), lambda j: (j, 0)),
        compiler_params=_cparams(("arbitrary",)),
        name="pack_w_in",
    )(w_in_t, w_in_t)


_NT = (((1,), (1,)), ((), ()))


def _proj_kernel(x_ref, mod_ref, ng_ref, wt_ref, wat_ref, p_ref, al_ref, h_ref):
    @pl.when(pl.program_id(1) == 0)
    def _():
        h = _modulated_norm(x_ref[...], ng_ref[0:1, :], mod_ref[0:1, :], mod_ref[1:2, :])
        hb = h.astype(BF16)
        h_ref[...] = hb
        al_ref[...] = lax.dot_general(hb, wat_ref[...], _NT, preferred_element_type=F32)

    p_ref[...] = lax.dot_general(h_ref[...], wt_ref[...], _NT,
                                 preferred_element_type=F32).astype(BF16)


def _mod_spec(l, tm):
    return pl.BlockSpec((None, None, 6, D_MODEL), lambda i, *_: (l, i * tm // SEQ, 0, 0))


def _gain_spec(l):
    return pl.BlockSpec((None, 4, D_MODEL), lambda *_: (l, 0, 0))


def _proj(l, x2, mod, norm_gains, w_main, w_alow):
    tm, tn = PROJ_TM, PROJ_TN
    return pl.pallas_call(
        _proj_kernel,
        out_shape=(jax.ShapeDtypeStruct((ROWS, P_COLS), BF16),
                   jax.ShapeDtypeStruct((ROWS, RANK_PAD), F32)),
        grid=(ROWS // tm, P_COLS // tn),
        in_specs=[
            pl.BlockSpec((tm, D_MODEL), lambda i, j: (i, 0)),
            _mod_spec(l, tm),
            _gain_spec(l),
            pl.BlockSpec((tn, D_MODEL), lambda i, j: (j, 0)),
            pl.BlockSpec((None, RANK_PAD, D_MODEL), lambda i, j: (l, 0, 0)),
        ],
        out_specs=(pl.BlockSpec((tm, tn), lambda i, j: (i, j)),
                   pl.BlockSpec((tm, RANK_PAD), lambda i, j: (i, 0))),
        scratch_shapes=[pltpu.VMEM((tm, D_MODEL), BF16)],
        compiler_params=_cparams(("arbitrary", "arbitrary")),
        name="proj",
    )(x2, mod, norm_gains, w_main, w_alow)


GLA_T = 512


def _gla_kernel(q_ref, k_ref, v_ref, r_ref, al_ref, wgu_ref, bg_ref, gn_ref, tri_ref,
                o_ref, st_ref):
    @pl.when(pl.program_id(2) == 0)
    def _():
        st_ref[...] = jnp.zeros_like(st_ref)

    a = jnp.dot(al_ref[...].astype(BF16), wgu_ref[...], preferred_element_type=F32) + bg_ref[...]
    log_a = -_softplus(-a) * (1.0 / GLA_GATE_TAU)
    tri = tri_ref[...]
    gain = gn_ref[...]

    n_chunks = GLA_T // CHUNK

    def chunk_rows(j):
        return slice(j * CHUNK, (j + 1) * CHUNK)

    def chunk_kv(j):
        rows = chunk_rows(j)
        hi, lo = _split_bf16(log_a[rows])
        g = (jnp.dot(tri, hi, preferred_element_type=F32)
             + jnp.dot(tri, lo, preferred_element_type=F32))
        g_tot = g[CHUNK - 1:CHUNK, :]
        kd = (k_ref[rows, :].astype(F32) * jnp.exp(g_tot - g)).astype(BF16)
        kv_t = lax.dot_general(v_ref[rows, :], kd, (((0,), (0,)), ((), ())),
                               preferred_element_type=F32)
        return jnp.exp(g_tot), kv_t

    ahead = 2
    pending = [chunk_kv(j) for j in range(ahead)]
    for j in range(n_chunks):
        decay, kv_t = pending.pop(0)
        if j + ahead < n_chunks:
            pending.append(chunk_kv(j + ahead))
        rows = chunk_rows(j)
        st = st_ref[...] * decay + kv_t
        st_ref[...] = st
        o = lax.dot_general(q_ref[rows, :], st.astype(BF16), _NT,
                            preferred_element_type=F32) * (GLA_DK ** -0.5)
        ms = jnp.mean(o * o, axis=-1, keepdims=True)
        on = o * lax.rsqrt(ms + EPS) * gain
        r = r_ref[rows, :].astype(F32)
        o_ref[rows, :] = (on * (r * jax.nn.sigmoid(r))).astype(BF16)


def _gla(l, p, a_low, wgu, bg, gn, tri):
    t = GLA_T
    nt = SEQ // t

    def row(b, h, i):
        return b * nt + i

    return pl.pallas_call(
        _gla_kernel,
        out_shape=jax.ShapeDtypeStruct((ROWS, GLA_V), BF16),
        grid=(BATCH, GLA_HEADS, nt),
        in_specs=[
            pl.BlockSpec((t, GLA_DK), lambda b, h, i: (row(b, h, i), COL_QA // GLA_DK + h)),
            pl.BlockSpec((t, GLA_DK), lambda b, h, i: (row(b, h, i), COL_KA // GLA_DK + h)),
            pl.BlockSpec((t, GLA_DV), lambda b, h, i: (row(b, h, i), COL_VA // GLA_DV + h)),
            pl.BlockSpec((t, GLA_DV), lambda b, h, i: (row(b, h, i), COL_RA // GLA_DV + h)),
            pl.BlockSpec((t, RANK_PAD), lambda b, h, i: (row(b, h, i), 0)),
            pl.BlockSpec((None, RANK_PAD, GLA_DK), lambda b, h, i: (l, 0, h)),
            pl.BlockSpec((None, 1, GLA_DK), lambda b, h, i: (l, 0, h)),
            pl.BlockSpec((None, None, 1, GLA_DV), lambda b, h, i: (l, h, 0, 0)),
            pl.BlockSpec((CHUNK, CHUNK), lambda b, h, i: (0, 0)),
        ],
        out_specs=pl.BlockSpec((t, GLA_DV), lambda b, h, i: (row(b, h, i), h)),
        scratch_shapes=[pltpu.VMEM((GLA_DV, GLA_DK), F32)],
        compiler_params=_cparams(("arbitrary", "arbitrary", "arbitrary")),
        name="gla",
    )(p, p, p, p, a_low, wgu, bg, gn, tri)


SB_TQ = 256
SB_TK = 256
SB_HB = 8
SB_HW = SB_HB * SB_DH
SB_MASKED = -1e30


def _sb_kernel(q_ref, k_ref, v_ref, tt_ref, o_ref, acc_ref, carry_ref,
               z0_ref, z1_ref, sp0_ref, sp1_ref, lb0_ref, lb1_ref):
    qi = pl.program_id(2)
    tt = tt_ref[...]
    scale = SB_DH ** -0.5 * LOG2E
    acc_ref[...] = jnp.zeros_like(acc_ref)
    carry_ref[...] = jnp.zeros_like(carry_ref)
    z_ref, sp_ref, lb_ref = (z0_ref, z1_ref), (sp0_ref, sp1_ref), (lb0_ref, lb1_ref)

    heads =[slice(h * SB_DH, (h + 1) * SB_DH) for h in range(SB_HB)]

    def keys(kb):
        return pl.ds(pl.multiple_of(kb * SB_TK, SB_TK), SB_TK)

    def step(w_kb=None, w_slot=None, s_slot=None, past=None, z_kb=None, z_slot=None):
        cs = [None] * SB_HB

        def matmuls(h):
            if w_kb is not None:
                cs[h] = jnp.dot(sp_ref[w_slot][h], tt, preferred_element_type=F32)
            if z_kb is not None:
                z_ref[z_slot][h] = lax.dot_general(q_ref[:, heads[h]], k_ref[keys(z_kb), heads[h]],
                                                   _NT, preferred_element_type=F32) * scale

        matmuls(0)
        for h in range(SB_HB):
            if h + 1 < SB_HB:
                matmuls(h + 1)
            if s_slot is not None:
                z = z_ref[s_slot][h]
                sp = jnp.maximum(z, 0.0) + jnp.log2(1.0 + jnp.exp2(_neg_abs(z)))
                log_beta = z - sp
                if past is not None:
                    sp = jnp.where(past, sp, 0.0)
                    log_beta = jnp.where(past, log_beta, SB_MASKED)
                sp_ref[s_slot][h] = sp.astype(BF16)
                lb_ref[s_slot][h] = log_beta
            if w_kb is not None:
                carry = carry_ref[h]
                w = jnp.exp2(lb_ref[w_slot][h] - (cs[h][:, :SB_TK] + carry)).astype(BF16)
                carry_ref[h] = carry + cs[h][:, SB_TK:]
                acc_ref[h] += jnp.dot(w, v_ref[keys(w_kb), heads[h]],
                                      preferred_element_type=F32)

    def block(j):
        return jnp.maximum(qi - j, 0)

    t_idx = lax.broadcasted_iota(jnp.int32, (SB_TQ, SB_TK), 0)
    s_idx = lax.broadcasted_iota(jnp.int32, (SB_TQ, SB_TK), 1)
    step(z_kb=qi, z_slot=0)
    step(s_slot=0, past=s_idx < t_idx, z_kb=block(1), z_slot=1)

    def pair(i, _):
        j = 2 * i
        step(w_kb=qi - j, w_slot=0, s_slot=1, z_kb=block(j + 2), z_slot=0)
        step(w_kb=qi - j - 1, w_slot=1, s_slot=0, z_kb=block(j + 3), z_slot=1)
        return 0

    lax.fori_loop(0, qi // 2, pair, 0)

    @pl.when(qi % 2 == 1)
    def _():
        step(w_kb=1, w_slot=0, s_slot=1)
        step(w_kb=0, w_slot=1)

    @pl.when(qi % 2 == 0)
    def _():
        step(w_kb=0, w_slot=0)

    for h in range(SB_HB):
        o_ref[:, h * SB_DH:(h + 1) * SB_DH] = acc_ref[h].astype(BF16)


def _sb(p, tt):
    nq = SEQ // SB_TQ
    return pl.pallas_call(
        _sb_kernel,
        out_shape=jax.ShapeDtypeStruct((ROWS, SB_W), BF16),
        grid=(BATCH, SB_HEADS // SB_HB, nq),
        in_specs=[
            pl.BlockSpec((SB_TQ, SB_HW), lambda b, h, i: (b * nq + i, COL_QB // SB_HW + h)),
            pl.BlockSpec((SEQ, SB_HW), lambda b, h, i: (b, COL_KB // SB_HW + h)),
            pl.BlockSpec((SEQ, SB_HW), lambda b, h, i: (b, COL_VB // SB_HW + h)),
            pl.BlockSpec((SB_TK, 2 * SB_TK), lambda b, h, i: (0, 0)),
        ],
        out_specs=pl.BlockSpec((SB_TQ, SB_HW), lambda b, h, i: (b * nq + i, h)),
        scratch_shapes=[pltpu.VMEM((SB_HB, SB_TQ, SB_DH), F32),
                        pltpu.VMEM((SB_HB, SB_TQ, SB_TK), F32),
                        pltpu.VMEM((SB_HB, SB_TQ, SB_TK), F32),
                        pltpu.VMEM((SB_HB, SB_TQ, SB_TK), F32),
                        pltpu.VMEM((SB_HB, SB_TQ, SB_TK), BF16),
                        pltpu.VMEM((SB_HB, SB_TQ, SB_TK), BF16),
                        pltpu.VMEM((SB_HB, SB_TQ, SB_TK), F32),
                        pltpu.VMEM((SB_HB, SB_TQ, SB_TK), F32)],
        compiler_params=_cparams(("arbitrary", "arbitrary", "arbitrary")),
        name="sb",
    )(p, p, p, tt)


MIX_TM = 1024
MIX_TN = 512


def _mix_kernel(oa_ref, ob_ref, wa_ref, wb_ref, ga_ref, gb_ref, o_ref):
    ya = jnp.dot(oa_ref[...], wa_ref[...], preferred_element_type=F32)
    yb = jnp.dot(ob_ref[...], wb_ref[...], preferred_element_type=F32)
    ga = jax.nn.sigmoid(ga_ref[...].astype(F32))
    gb = jax.nn.sigmoid(gb_ref[...].astype(F32))
    o_ref[...] = (ga * ya + gb * yb).astype(BF16)


def _mix(o_gla, o_sb, w_gla_o, w_sb_o, p):
    tm, tn = MIX_TM, MIX_TN
    return pl.pallas_call(
        _mix_kernel,
        out_shape=jax.ShapeDtypeStruct((ROWS, D_MODEL), BF16),
        grid=(ROWS // tm, D_MODEL // tn),
        in_specs=[
            pl.BlockSpec((tm, GLA_V), lambda i, j: (i, 0)),
            pl.BlockSpec((tm, SB_W), lambda i, j: (i, 0)),
            pl.BlockSpec((GLA_V, tn), lambda i, j: (0, j)),
            pl.BlockSpec((SB_W, tn), lambda i, j: (0, j)),
            pl.BlockSpec((tm, tn), lambda i, j: (i, COL_GA // tn + j)),
            pl.BlockSpec((tm, tn), lambda i, j: (i, COL_GB // tn + j)),
        ],
        out_specs=pl.BlockSpec((tm, tn), lambda i, j: (i, j)),
        compiler_params=_cparams(("arbitrary", "arbitrary")),
        name="mix",
    )(o_gla, o_sb, w_gla_o, w_sb_o, p, p)


OUT_TM = 512


def _out_kernel(m_ref, w_ref, x_ref, mod_ref, ng_ref, o_ref):
    y = jnp.dot(m_ref[...], w_ref[...], preferred_element_type=F32)
    ms = jnp.mean(y * y, axis=-1, keepdims=True)
    yn = y * lax.rsqrt(ms + EPS) * ng_ref[1:2, :]
    o_ref[...] = x_ref[...] + mod_ref[2:3, :] * yn


def _out(l, mixed, w_out, x2, mod, norm_gains):
    tm = OUT_TM
    return pl.pallas_call(
        _out_kernel,
        out_shape=jax.ShapeDtypeStruct((ROWS, D_MODEL), F32),
        grid=(ROWS // tm,),
        in_specs=[
            pl.BlockSpec((tm, D_MODEL), lambda i: (i, 0)),
            pl.BlockSpec((D_MODEL, D_MODEL), lambda i: (0, 0)),
            pl.BlockSpec((tm, D_MODEL), lambda i: (i, 0)),
            _mod_spec(l, tm),
            _gain_spec(l),
        ],
        out_specs=pl.BlockSpec((tm, D_MODEL), lambda i: (i, 0)),
        compiler_params=_cparams(("arbitrary",)),
        name="out",
    )(mixed, w_out, x2, mod, norm_gains)


FFN_TM = 512
FFN_TF = 1024


FFN_STEPS = (ROWS // FFN_TM) * (D_FF // FFN_TF)
CAST_IN_ROWS = P_COLS // FFN_STEPS
CAST_IN_LO_TILES = A_LOW_START // CAST_IN_ROWS


N_PLAIN_CASTS = 5


def _ffn_cast_kernel(x_ref, mod_ref, ng_ref, w1_ref, w2_ref, win_ref, winx_ref, *rest):
    plain_in = rest[:N_PLAIN_CASTS]
    o_ref, win_o = rest[N_PLAIN_CASTS:N_PLAIN_CASTS + 2]
    plain_out = rest[N_PLAIN_CASTS + 2:2 * N_PLAIN_CASTS + 2]
    h_ref, acc_ref = rest[2 * N_PLAIN_CASTS + 2:]

    def cast_slabs():
        step = pl.program_id(0) * pl.num_programs(1) + pl.program_id(1)
        win_o[...] = _packed_rows(step, win_ref[...], winx_ref[...], CAST_IN_ROWS, CAST_IN_LO_TILES)
        for src, dst in zip(plain_in, plain_out):
            dst[...] = src[...].astype(BF16)

    _ffn_kernel(x_ref, mod_ref, ng_ref, w1_ref, w2_ref, o_ref, h_ref, acc_ref, beside_dots=cast_slabs)


def _ffn_kernel(x_ref, mod_ref, ng_ref, w1_ref, w2_ref, o_ref, h_ref, acc_ref, beside_dots=None):
    f = pl.program_id(1)

    @pl.when(f == 0)
    def _():
        h = _modulated_norm(x_ref[...], ng_ref[2:3, :], mod_ref[3:4, :], mod_ref[4:5, :])
        h_ref[...] = h.astype(BF16)
        acc_ref[...] = jnp.zeros_like(acc_ref)

    if beside_dots is not None:
        beside_dots()
    a = jnp.dot(h_ref[...], w1_ref[...], preferred_element_type=F32)
    a = jnp.square(jnp.maximum(a, 0.0)).astype(BF16)
    acc_ref[...] += jnp.dot(a, w2_ref[...], preferred_element_type=F32)

    @pl.when(f == pl.num_programs(1) - 1)
    def _():
        y = acc_ref[...]
        ms = jnp.mean(y * y, axis=-1, keepdims=True)
        yn = y * lax.rsqrt(ms + EPS) * ng_ref[3:4, :]
        o_ref[...] = x_ref[...] + mod_ref[5:6, :] * yn


def _ffn(l, x2, mod, norm_gains, w1, w2, next_f32=None):
    tm, tf = FFN_TM, FFN_TF
    nf = D_FF // tf
    in_specs = [
        pl.BlockSpec((tm, D_MODEL), lambda i, f: (i, 0)),
        _mod_spec(l, tm),
        _gain_spec(l),
        pl.BlockSpec((D_MODEL, tf), lambda i, f: (0, f)),
        pl.BlockSpec((tf, D_MODEL), lambda i, f: (f, 0)),
    ]
    x_spec = pl.BlockSpec((tm, D_MODEL), lambda i, f: (i, 0))
    x_shape = jax.ShapeDtypeStruct((ROWS, D_MODEL), F32)
    scratch = [pltpu.VMEM((tm, D_MODEL), BF16), pltpu.VMEM((tm, D_MODEL), F32)]
    params = _cparams(("arbitrary", "arbitrary"))
    if next_f32 is None:
        return pl.pallas_call(
            _ffn_kernel, out_shape=x_shape, grid=(ROWS // tm, nf), in_specs=in_specs,
            out_specs=x_spec, scratch_shapes=scratch, compiler_params=params, name="ffn",
        )(x2, mod, norm_gains, w1, w2)

    def step(i, f):
        return i * nf + f

    def slab(rows, cols):
        return pl.BlockSpec((rows, cols), lambda i, f: (step(i, f), 0))

    def slab_of_next(rows, cols):
        return pl.BlockSpec((None, rows, cols), lambda i, f: (l + 1, step(i, f), 0))

    w_in_t, plain = next_f32[0], next_f32[1:]
    assert len(plain) == N_PLAIN_CASTS
    shapes = [w.shape[1:] for w in plain]
    slabs = [(r // FFN_STEPS, c) for r, c in shapes]
    return pl.pallas_call(
        _ffn_cast_kernel,
        out_shape=(x_shape, jax.ShapeDtypeStruct((P_COLS, D_MODEL), BF16),
                   *[jax.ShapeDtypeStruct(s, BF16) for s in shapes]),
        grid=(ROWS // tm, nf),
        in_specs=(in_specs + _pack_specs(l + 1, CAST_IN_ROWS, step)
                  + [slab_of_next(r, c) for r, c in slabs]),
        out_specs=(x_spec, slab(CAST_IN_ROWS, D_MODEL), *[slab(r, c) for r, c in slabs]),
        scratch_shapes=scratch, compiler_params=params, name="ffn",
    )(x2, mod, norm_gains, w1, w2, w_in_t, w_in_t, *plain)


def _cumsum_constants():
    j = np.arange(CHUNK)
    tri = (j[:, None] >= j[None, :]).astype(np.float32)
    jj = np.arange(SB_TK)
    s = np.arange(2 * SB_TK)
    tt = ((jj[:, None] > s[None, :]) | (s[None, :] >= SB_TK)).astype(np.float32)
    return jnp.asarray(tri, BF16), jnp.asarray(tt, BF16)


def kernel(x, c, w_ada, b_ada, norm_gains, w_in, w_gate_up, b_gate, gla_norm_gain,
           w_gla_o, w_sb_o, w_out, w_ff1, w_ff2):
    tri, tt = _cumsum_constants()

    c_pad = jnp.pad(c, ((0, 8 - BATCH), (0, 0)))
    mod = _ada(c_pad, w_ada, b_ada.reshape(DEPTH, 1, 6 * D_MODEL))
    mod = mod[:, :BATCH, :].reshape(DEPTH, BATCH, 6, D_MODEL)

    w_in_t = jnp.swapaxes(w_in, 1, 2)
    w_alow = jnp.pad(w_in_t[:, A_LOW_START:A_LOW_END, :],
                     ((0, 0), (0, RANK_PAD - GLA_GATE_RANK), (0, 0))).astype(BF16)
    wgu = jnp.pad(w_gate_up, ((0, 0), (0, RANK_PAD - GLA_GATE_RANK), (0, 0))).astype(BF16)
    bg = b_gate.reshape(DEPTH, 1, GLA_K)
    gn = gla_norm_gain.reshape(DEPTH, GLA_HEADS, 1, GLA_DV)
    plain_f32 = (w_gla_o, w_sb_o, w_out, w_ff1, w_ff2)
    w_main = _pack_w_in(0, w_in_t)
    w_gla_o_b, w_sb_o_b, w_out_b, w_ff1_b, w_ff2_b = (w[0].astype(BF16) for w in plain_f32)

    x2 = x.reshape(ROWS, D_MODEL)
    for l in range(DEPTH):
        p, a_low = _proj(l, x2, mod, norm_gains, w_main, w_alow)
        o_gla = _gla(l, p, a_low, wgu, bg, gn, tri)
        o_sb = _sb(p, tt)
        mixed = _mix(o_gla, o_sb, w_gla_o_b, w_sb_o_b, p)
        x2 = _out(l, mixed, w_out_b, x2, mod, norm_gains)
        if l + 1 < DEPTH:
            x2, w_main, w_gla_o_b, w_sb_o_b, w_out_b, w_ff1_b, w_ff2_b = _ffn(
                l, x2, mod, norm_gains, w_ff1_b, w_ff2_b, next_f32=(w_in_t,) + plain_f32)
        else:
            x2 = _ffn(l, x2, mod, norm_gains, w_ff1_b, w_ff2_b)
    return x2.reshape(BATCH, SEQ, D_MODEL)
```

```python
import numpy as np
import jax
import jax.numpy as jnp
from jax import lax
from jax.experimental import pallas as pl
from jax.experimental.pallas import tpu as pltpu

F32 = jnp.float32
BF16 = jnp.bfloat16

D_MODEL = 2048
BATCH = 4
SEQ = 2048
DEPTH = 4
ROWS = BATCH * SEQ

CHUNK = 64
GLA_HEADS = 4
GLA_DK = 256
GLA_DV = 512
GLA_K = GLA_HEADS * GLA_DK
GLA_V = GLA_HEADS * GLA_DV
GLA_GATE_RANK = 16
GLA_GATE_TAU = 16.0
SB_HEADS = 16
SB_DH = 128
SB_W = SB_HEADS * SB_DH
D_FF = 4 * D_MODEL
EPS = 1e-6

LOG2E = 1.4426950408889634
LANES = 128
RANK_PAD = LANES
A_LOW_START = 2 * GLA_K + 2 * GLA_V
A_LOW_END = A_LOW_START + GLA_GATE_RANK

P_COLS = 2 * GLA_K + 2 * GLA_V + 3 * SB_W + 2 * D_MODEL
COL_QA = 0
COL_KA = GLA_K
COL_VA = 2 * GLA_K
COL_RA = 2 * GLA_K + GLA_V
COL_QB = 2 * GLA_K + 2 * GLA_V
COL_KB = COL_QB + SB_W
COL_VB = COL_KB + SB_W
COL_GA = COL_VB + SB_W
COL_GB = COL_GA + D_MODEL

VMEM_LIMIT = 56 * 1024 * 1024


def _cparams(sem):
    return pltpu.CompilerParams(dimension_semantics=sem, vmem_limit_bytes=VMEM_LIMIT)


def _softplus(z):
    return jnp.maximum(z, 0.0) + jnp.log(1.0 + jnp.exp(-jnp.abs(z)))


def _split_bf16(a):
    hi = a.astype(BF16)
    lo = (a - hi.astype(F32)).astype(BF16)
    return hi, lo


ADA_TK = 256


def _ada_kernel(c_ref, w_ref, b_ref, o_ref):
    @pl.when(pl.program_id(1) == 0)
    def _():
        o_ref[...] = jnp.broadcast_to(b_ref[...], o_ref.shape)

    c = c_ref[...]
    ca = (c * jax.nn.sigmoid(c)).astype(BF16)
    o_ref[...] += jnp.dot(ca, w_ref[...].astype(BF16), preferred_element_type=F32)


def _ada(c_pad, w_ada, b_ada3):
    n = 6 * D_MODEL
    return pl.pallas_call(
        _ada_kernel,
        out_shape=jax.ShapeDtypeStruct((DEPTH, 8, n), F32),
        grid=(DEPTH, D_MODEL // ADA_TK),
        in_specs=[
            pl.BlockSpec((8, ADA_TK), lambda l, k: (0, k)),
            pl.BlockSpec((None, ADA_TK, n), lambda l, k: (l, k, 0)),
            pl.BlockSpec((None, 1, n), lambda l, k: (l, 0, 0)),
        ],
        out_specs=pl.BlockSpec((None, 8, n), lambda l, k: (l, 0, 0)),
        compiler_params=_cparams(("arbitrary", "arbitrary")),
        name="ada",
    )(c_pad, w_ada, b_ada3)


PROJ_TM = 1024
PROJ_TN = 2048
PACK_TN = 1024
PACK_LO_TILES = A_LOW_START // PACK_TN


def _modulated_norm(x, gain, shift, scale):
    ms = jnp.mean(x * x, axis=-1, keepdims=True)
    y = x * lax.rsqrt(ms + EPS) * gain
    return y * (1.0 + scale) + shift


def _packed_rows(tile, w, nxt, rows, lo_tiles):
    shifted = jnp.concatenate([w[GLA_GATE_RANK:rows, :], nxt], axis=0)
    return jnp.where(tile < lo_tiles, w, shifted).astype(BF16)


def _pack_specs(l, rows, tile_of):
    per = rows // GLA_GATE_RANK
    return [pl.BlockSpec((None, rows, D_MODEL), lambda *g: (l, tile_of(*g), 0)),
            pl.BlockSpec((None, GLA_GATE_RANK, D_MODEL),
                         lambda *g: (l, (tile_of(*g) + 1) * per, 0))]


def _pack_kernel(w_ref, nxt_ref, o_ref):
    o_ref[...] = _packed_rows(pl.program_id(0), w_ref[...], nxt_ref[...], PACK_TN, PACK_LO_TILES)


def _pack_w_in(l, w_in_t):
    tn = PACK_TN
    return pl.pallas_call(
        _pack_kernel,
        out_shape=jax.ShapeDtypeStruct((P_COLS, D_MODEL), BF16),
        grid=(P_COLS // tn,),
        in_specs=_pack_specs(l, tn, lambda j: j),
        out_specs=pl.BlockSpec((tn, D_MODEL), lambda j: (j, 0)),
        compiler_params=_cparams(("arbitrary",)),
        name="pack_w_in",
    )(w_in_t, w_in_t)


_NT = (((1,), (1,)), ((), ()))


def _proj_kernel(x_ref, mod_ref, ng_ref, wt_ref, wat_ref, p_ref, al_ref, h_ref):
    @pl.when(pl.program_id(1) == 0)
    def _():
        h = _modulated_norm(x_ref[...], ng_ref[0:1, :], mod_ref[0:1, :], mod_ref[1:2, :])
        hb = h.astype(BF16)
        h_ref[...] = hb
        al_ref[...] = lax.dot_general(hb, wat_ref[...], _NT, preferred_element_type=F32)

    p_ref[...] = lax.dot_general(h_ref[...], wt_ref[...], _NT,
                                 preferred_element_type=F32).astype(BF16)


def _mod_spec(l, tm):
    return pl.BlockSpec((None, None, 6, D_MODEL), lambda i, *_: (l, i * tm // SEQ, 0, 0))


def _gain_spec(l):
    return pl.BlockSpec((None, 4, D_MODEL), lambda *_: (l, 0, 0))


def _proj(l, x2, mod, norm_gains, w_main, w_alow):
    tm, tn = PROJ_TM, PROJ_TN
    return pl.pallas_call(
        _proj_kernel,
        out_shape=(jax.ShapeDtypeStruct((ROWS, P_COLS), BF16),
                   jax.ShapeDtypeStruct((ROWS, RANK_PAD), F32)),
        grid=(ROWS // tm, P_COLS // tn),
        in_specs=[
            pl.BlockSpec((tm, D_MODEL), lambda i, j: (i, 0)),
            _mod_spec(l, tm),
            _gain_spec(l),
            pl.BlockSpec((tn, D_MODEL), lambda i, j: (j, 0)),
            pl.BlockSpec((None, RANK_PAD, D_MODEL), lambda i, j: (l, 0, 0)),
        ],
        out_specs=(pl.BlockSpec((tm, tn), lambda i, j: (i, j)),
                   pl.BlockSpec((tm, RANK_PAD), lambda i, j: (i, 0))),
        scratch_shapes=[pltpu.VMEM((tm, D_MODEL), BF16)],
        compiler_params=_cparams(("arbitrary", "arbitrary")),
        name="proj",
    )(x2, mod, norm_gains, w_main, w_alow)


GLA_T = 512
GLA_HB = 4


def _gla_kernel(q_ref, k_ref, v_ref, r_ref, al_ref, wgu_ref, bg_ref, gn_ref, tri_ref,
                o_ref, st_ref):
    @pl.when(pl.program_id(2) == 0)
    def _():
        st_ref[...] = jnp.zeros_like(st_ref)

    a = jnp.dot(al_ref[...].astype(BF16), wgu_ref[...], preferred_element_type=F32) + bg_ref[...]
    log_a = -_softplus(-a) * (1.0 / GLA_GATE_TAU)
    tri = tri_ref[...]

    n_chunks = GLA_T // CHUNK

    def chunk_rows(j):
        return slice(j * CHUNK, (j + 1) * CHUNK)

    def chunk_kv(j, h):
        rows = chunk_rows(j)
        kcols = slice(h * GLA_DK, (h + 1) * GLA_DK)
        hi, lo = _split_bf16(log_a[rows, kcols])
        g = (jnp.dot(tri, hi, preferred_element_type=F32)
             + jnp.dot(tri, lo, preferred_element_type=F32))
        g_tot = g[CHUNK - 1:CHUNK, :]
        kd = (k_ref[rows, kcols].astype(F32) * jnp.exp(g_tot - g)).astype(BF16)
        kv_t = lax.dot_general(v_ref[rows, h * GLA_DV:(h + 1) * GLA_DV], kd,
                               (((0,), (0,)), ((), ())), preferred_element_type=F32)
        return jnp.exp(g_tot), kv_t

    ahead = 2
    pending = [[chunk_kv(j, h) for j in range(ahead)] for h in range(GLA_HB)]
    for j in range(n_chunks):
        rows = chunk_rows(j)
        for h in range(GLA_HB):
            vcols = slice(h * GLA_DV, (h + 1) * GLA_DV)
            decay, kv_t = pending[h].pop(0)
            if j + ahead < n_chunks:
                pending[h].append(chunk_kv(j + ahead, h))
            st = st_ref[h] * decay + kv_t
            st_ref[h] = st
            o = lax.dot_general(q_ref[rows, h * GLA_DK:(h + 1) * GLA_DK], st.astype(BF16), _NT,
                                preferred_element_type=F32) * (GLA_DK ** -0.5)
            ms = jnp.mean(o * o, axis=-1, keepdims=True)
            on = o * lax.rsqrt(ms + EPS) * gn_ref[h]
            r = r_ref[rows, vcols].astype(F32)
            o_ref[rows, vcols] = (on * (r * jax.nn.sigmoid(r))).astype(BF16)


def _gla(l, p, a_low, wgu, bg, gn, tri):
    t, hb = GLA_T, GLA_HB
    nt = SEQ // t
    wk, wv = hb * GLA_DK, hb * GLA_DV

    def row(b, h, i):
        return b * nt + i

    return pl.pallas_call(
        _gla_kernel,
        out_shape=jax.ShapeDtypeStruct((ROWS, GLA_V), BF16),
        grid=(BATCH, GLA_HEADS // hb, nt),
        in_specs=[
            pl.BlockSpec((t, wk), lambda b, h, i: (row(b, h, i), COL_QA // wk + h)),
            pl.BlockSpec((t, wk), lambda b, h, i: (row(b, h, i), COL_KA // wk + h)),
            pl.BlockSpec((t, wv), lambda b, h, i: (row(b, h, i), COL_VA // wv + h)),
            pl.BlockSpec((t, wv), lambda b, h, i: (row(b, h, i), COL_RA // wv + h)),
            pl.BlockSpec((t, RANK_PAD), lambda b, h, i: (row(b, h, i), 0)),
            pl.BlockSpec((None, RANK_PAD, wk), lambda b, h, i: (l, 0, h)),
            pl.BlockSpec((None, 1, wk), lambda b, h, i: (l, 0, h)),
            pl.BlockSpec((None, hb, 1, GLA_DV), lambda b, h, i: (l, h, 0, 0)),
            pl.BlockSpec((CHUNK, CHUNK), lambda b, h, i: (0, 0)),
        ],
        out_specs=pl.BlockSpec((t, wv), lambda b, h, i: (row(b, h, i), h)),
        scratch_shapes=[pltpu.VMEM((hb, GLA_DV, GLA_DK), F32)],
        compiler_params=_cparams(("arbitrary", "arbitrary", "arbitrary")),
        name="gla",
    )(p, p, p, p, a_low, wgu, bg, gn, tri)


SB_TQ = 256
SB_TK = 256
SB_HB = 8
SB_HW = SB_HB * SB_DH
SB_MASKED = -1e30


def _sb_kernel(q_ref, k_ref, v_ref, tt_ref, o_ref, acc_ref, carry_ref, z_ref, sp_ref, lb_ref):
    qi = pl.program_id(2)
    tt = tt_ref[...]
    scale = SB_DH ** -0.5 * LOG2E
    acc_ref[...] = jnp.zeros_like(acc_ref)
    carry_ref[...] = jnp.zeros_like(carry_ref)

    heads = [slice(h * SB_DH, (h + 1) * SB_DH) for h in range(SB_HB)]

    def keys(kb):
        return pl.ds(pl.multiple_of(kb * SB_TK, SB_TK), SB_TK)

    def step(w_kb=None, w_slot=None, s_slot=None, past=None, z_kb=None, z_slot=None):
        cs = [None] * SB_HB

        def matmuls(h):
            if w_kb is not None:
                cs[h] = jnp.dot(sp_ref[w_slot, h], tt, preferred_element_type=F32)
            if z_kb is not None:
                z_ref[z_slot, h] = lax.dot_general(q_ref[:, heads[h]], k_ref[keys(z_kb), heads[h]],
                                                   _NT, preferred_element_type=F32) * scale

        matmuls(0)
        for h in range(SB_HB):
            if h + 1 < SB_HB:
                matmuls(h + 1)
            if s_slot is not None:
                z = z_ref[s_slot, h]
                sp = jnp.maximum(z, 0.0) + jnp.log2(1.0 + jnp.exp2(-jnp.abs(z)))
                log_beta = z - sp
                if past is not None:
                    sp = jnp.where(past, sp, 0.0)
                    log_beta = jnp.where(past, log_beta, SB_MASKED)
                sp_ref[s_slot, h] = sp.astype(BF16)
                lb_ref[s_slot, h] = log_beta
            if w_kb is not None:
                carry = carry_ref[h]
                w = jnp.exp2(lb_ref[w_slot, h] - (cs[h][:, :SB_TK] + carry)).astype(BF16)
                carry_ref[h] = carry + cs[h][:, SB_TK:]
                acc_ref[h] += jnp.dot(w, v_ref[keys(w_kb), heads[h]],
                                      preferred_element_type=F32)

    def block(j):
        return jnp.maximum(qi - j, 0)

    t_idx = lax.broadcasted_iota(jnp.int32, (SB_TQ, SB_TK), 0)
    s_idx = lax.broadcasted_iota(jnp.int32, (SB_TQ, SB_TK), 1)
    step(z_kb=qi, z_slot=0)
    step(s_slot=0, past=s_idx < t_idx, z_kb=block(1), z_slot=1)

    def pair(i, _):
        j = 2 * i
        step(w_kb=qi - j, w_slot=0, s_slot=1, z_kb=block(j + 2), z_slot=0)
        step(w_kb=qi - j - 1, w_slot=1, s_slot=0, z_kb=block(j + 3), z_slot=1)
        return 0

    lax.fori_loop(0, qi // 2, pair, 0)

    @pl.when(qi % 2 == 1)
    def _():
        step(w_kb=1, w_slot=0, s_slot=1)
        step(w_kb=0, w_slot=1)

    @pl.when(qi % 2 == 0)
    def _():
        step(w_kb=0, w_slot=0)

    for h in range(SB_HB):
        o_ref[:, h * SB_DH:(h + 1) * SB_DH] = acc_ref[h].astype(BF16)


def _sb(p, tt):
    nq = SEQ // SB_TQ
    return pl.pallas_call(
        _sb_kernel,
        out_shape=jax.ShapeDtypeStruct((ROWS, SB_W), BF16),
        grid=(BATCH, SB_HEADS // SB_HB, nq),
        in_specs=[
            pl.BlockSpec((SB_TQ, SB_HW), lambda b, h, i: (b * nq + i, COL_QB // SB_HW + h)),
            pl.BlockSpec((SEQ, SB_HW), lambda b, h, i: (b, COL_KB // SB_HW + h)),
            pl.BlockSpec((SEQ, SB_HW), lambda b, h, i: (b, COL_VB // SB_HW + h)),
            pl.BlockSpec((SB_TK, 2 * SB_TK), lambda b, h, i: (0, 0)),
        ],
        out_specs=pl.BlockSpec((SB_TQ, SB_HW), lambda b, h, i: (b * nq + i, h)),
        scratch_shapes=[pltpu.VMEM((SB_HB, SB_TQ, SB_DH), F32),
                        pltpu.VMEM((SB_HB, SB_TQ, SB_TK), F32),
                        pltpu.VMEM((2, SB_HB, SB_TQ, SB_TK), F32),
                        pltpu.VMEM((2, SB_HB, SB_TQ, SB_TK), BF16),
                        pltpu.VMEM((2, SB_HB, SB_TQ, SB_TK), F32)],
        compiler_params=_cparams(("arbitrary", "arbitrary", "arbitrary")),
        name="sb",
    )(p, p, p, tt)


MIX_TM = 1024
MIX_TN = 512


def _mix_kernel(oa_ref, ob_ref, wa_ref, wb_ref, ga_ref, gb_ref, o_ref):
    ga = jax.nn.sigmoid(ga_ref[...].astype(F32))
    gb = jax.nn.sigmoid(gb_ref[...].astype(F32))
    ya = jnp.dot(oa_ref[...], wa_ref[...], preferred_element_type=F32)
    yb = jnp.dot(ob_ref[...], wb_ref[...], preferred_element_type=F32)
    o_ref[...] = (ga * ya + gb * yb).astype(BF16)


def _mix(o_gla, o_sb, w_gla_o, w_sb_o, p):
    tm, tn = MIX_TM, MIX_TN
    return pl.pallas_call(
        _mix_kernel,
        out_shape=jax.ShapeDtypeStruct((ROWS, D_MODEL), BF16),
        grid=(ROWS // tm, D_MODEL // tn),
        in_specs=[
            pl.BlockSpec((tm, GLA_V), lambda i, j: (i, 0)),
            pl.BlockSpec((tm, SB_W), lambda i, j: (i, 0)),
            pl.BlockSpec((GLA_V, tn), lambda i, j: (0, j)),
            pl.BlockSpec((SB_W, tn), lambda i, j: (0, j)),
            pl.BlockSpec((tm, tn), lambda i, j: (i, COL_GA // tn + j)),
            pl.BlockSpec((tm, tn), lambda i, j: (i, COL_GB // tn + j)),
        ],
        out_specs=pl.BlockSpec((tm, tn), lambda i, j: (i, j)),
        compiler_params=_cparams(("arbitrary", "arbitrary")),
        name="mix",
    )(o_gla, o_sb, w_gla_o, w_sb_o, p, p)


OUT_TM = 512


def _out_kernel(m_ref, w_ref, x_ref, mod_ref, ng_ref, o_ref):
    y = jnp.dot(m_ref[...], w_ref[...], preferred_element_type=F32)
    ms = jnp.mean(y * y, axis=-1, keepdims=True)
    yn = y * lax.rsqrt(ms + EPS) * ng_ref[1:2, :]
    o_ref[...] = x_ref[...] + mod_ref[2:3, :] * yn


def _out(l, mixed, w_out, x2, mod, norm_gains):
    tm = OUT_TM
    return pl.pallas_call(
        _out_kernel,
        out_shape=jax.ShapeDtypeStruct((ROWS, D_MODEL), F32),
        grid=(ROWS // tm,),
        in_specs=[
            pl.BlockSpec((tm, D_MODEL), lambda i: (i, 0)),
            pl.BlockSpec((D_MODEL, D_MODEL), lambda i: (0, 0)),
            pl.BlockSpec((tm, D_MODEL), lambda i: (i, 0)),
            _mod_spec(l, tm),
            _gain_spec(l),
        ],
        out_specs=pl.BlockSpec((tm, D_MODEL), lambda i: (i, 0)),
        compiler_params=_cparams(("arbitrary",)),
        name="out",
    )(mixed, w_out, x2, mod, norm_gains)


FFN_TM = 512
FFN_TF = 1024
FFN_STEPS = (ROWS // FFN_TM) * (D_FF // FFN_TF)
CAST_IN_ROWS = P_COLS // FFN_STEPS
CAST_IN_LO_TILES = A_LOW_START // CAST_IN_ROWS
N_PLAIN_CASTS = 5


def _ffn_cast_kernel(x_ref, mod_ref, ng_ref, w1_ref, w2_ref, win_ref, winx_ref, *rest):
    plain_in = rest[:N_PLAIN_CASTS]
    o_ref, win_o = rest[N_PLAIN_CASTS:N_PLAIN_CASTS + 2]
    plain_out = rest[N_PLAIN_CASTS + 2:2 * N_PLAIN_CASTS + 2]
    h_ref, acc_ref = rest[2 * N_PLAIN_CASTS + 2:]

    def cast_slabs():
        step = pl.program_id(0) * pl.num_programs(1) + pl.program_id(1)
        win_o[...] = _packed_rows(step, win_ref[...], winx_ref[...], CAST_IN_ROWS, CAST_IN_LO_TILES)
        for src, dst in zip(plain_in, plain_out):
            dst[...] = src[...].astype(BF16)

    _ffn_kernel(x_ref, mod_ref, ng_ref, w1_ref, w2_ref, o_ref, h_ref, acc_ref, beside_dots=cast_slabs)


def _ffn_kernel(x_ref, mod_ref, ng_ref, w1_ref, w2_ref, o_ref, h_ref, acc_ref, beside_dots=None):
    f = pl.program_id(1)

    @pl.when(f == 0)
    def _():
        h = _modulated_norm(x_ref[...], ng_ref[2:3, :], mod_ref[3:4, :], mod_ref[4:5, :])
        h_ref[...] = h.astype(BF16)
        acc_ref[...] = jnp.zeros_like(acc_ref)

    if beside_dots is not None:
        beside_dots()
    a = jnp.dot(h_ref[...], w1_ref[...], preferred_element_type=F32)
    a = jnp.square(jnp.maximum(a, 0.0)).astype(BF16)
    acc_ref[...] += jnp.dot(a, w2_ref[...], preferred_element_type=F32)

    @pl.when(f == pl.num_programs(1) - 1)
    def _():
        y = acc_ref[...]
        ms = jnp.mean(y * y, axis=-1, keepdims=True)
        yn = y * lax.rsqrt(ms + EPS) * ng_ref[3:4, :]
        o_ref[...] = x_ref[...] + mod_ref[5:6, :] * yn


def _ffn(l, x2, mod, norm_gains, w1, w2, next_f32=None):
    tm, tf = FFN_TM, FFN_TF
    nf = D_FF // tf
    in_specs = [
        pl.BlockSpec((tm, D_MODEL), lambda i, f: (i, 0)),
        _mod_spec(l, tm),
        _gain_spec(l),
        pl.BlockSpec((D_MODEL, tf), lambda i, f: (0, f)),
        pl.BlockSpec((tf, D_MODEL), lambda i, f: (f, 0)),
    ]
    x_spec = pl.BlockSpec((tm, D_MODEL), lambda i, f: (i, 0))
    x_shape = jax.ShapeDtypeStruct((ROWS, D_MODEL), F32)
    scratch = [pltpu.VMEM((tm, D_MODEL), BF16), pltpu.VMEM((tm, D_MODEL), F32)]
    params = _cparams(("arbitrary", "arbitrary"))
    if next_f32 is None:
        return pl.pallas_call(
            _ffn_kernel, out_shape=x_shape, grid=(ROWS // tm, nf), in_specs=in_specs,
            out_specs=x_spec, scratch_shapes=scratch, compiler_params=params, name="ffn",
        )(x2, mod, norm_gains, w1, w2)

    def step(i, f):
        return i * nf + f

    def slab(rows, cols):
        return pl.BlockSpec((rows, cols), lambda i, f: (step(i, f), 0))

    def slab_of_next(rows, cols):
        return pl.BlockSpec((None, rows, cols), lambda i, f: (l + 1, step(i, f), 0))

    w_in_t, plain = next_f32[0], next_f32[1:]
    assert len(plain) == N_PLAIN_CASTS
    shapes = [w.shape[1:] for w in plain]
    slabs = [(r // FFN_STEPS, c) for r, c in shapes]
    return pl.pallas_call(
        _ffn_cast_kernel,
        out_shape=(x_shape, jax.ShapeDtypeStruct((P_COLS, D_MODEL), BF16),
                   *[jax.ShapeDtypeStruct(s, BF16) for s in shapes]),
        grid=(ROWS // tm, nf),
        in_specs=(in_specs + _pack_specs(l + 1, CAST_IN_ROWS, step)
                  + [slab_of_next(r, c) for r, c in slabs]),
        out_specs=(x_spec, slab(CAST_IN_ROWS, D_MODEL), *[slab(r, c) for r, c in slabs]),
        scratch_shapes=scratch, compiler_params=params, name="ffn",
    )(x2, mod, norm_gains, w1, w2, w_in_t, w_in_t, *plain)


def _cumsum_constants():
    j = np.arange(CHUNK)
    tri = (j[:, None] >= j[None, :]).astype(np.float32)
    jj = np.arange(SB_TK)
    s = np.arange(2 * SB_TK)
    tt = ((jj[:, None] > s[None, :]) | (s[None, :] >= SB_TK)).astype(np.float32)
    return jnp.asarray(tri, BF16), jnp.asarray(tt, BF16)


def kernel(x, c, w_ada, b_ada, norm_gains, w_in, w_gate_up, b_gate, gla_norm_gain,
           w_gla_o, w_sb_o, w_out, w_ff1, w_ff2):
    tri, tt = _cumsum_constants()

    c_pad = jnp.pad(c, ((0, 8 - BATCH), (0, 0)))
    mod = _ada(c_pad, w_ada, b_ada.reshape(DEPTH, 1, 6 * D_MODEL))
    mod = mod[:, :BATCH, :].reshape(DEPTH, BATCH, 6, D_MODEL)

    w_in_t = jnp.swapaxes(w_in, 1, 2)
    w_alow = jnp.pad(w_in_t[:, A_LOW_START:A_LOW_END, :],
                     ((0, 0), (0, RANK_PAD - GLA_GATE_RANK), (0, 0))).astype(BF16)
    wgu = jnp.pad(w_gate_up, ((0, 0), (0, RANK_PAD - GLA_GATE_RANK), (0, 0))).astype(BF16)
    bg = b_gate.reshape(DEPTH, 1, GLA_K)
    gn = gla_norm_gain.reshape(DEPTH, GLA_HEADS, 1, GLA_DV)
    plain_f32 = (w_gla_o, w_sb_o, w_out, w_ff1, w_ff2)
    w_main = _pack_w_in(0, w_in_t)
    w_gla_o_b, w_sb_o_b, w_out_b, w_ff1_b, w_ff2_b = (w[0].astype(BF16) for w in plain_f32)

    x2 = x.reshape(ROWS, D_MODEL)
    for l in range(DEPTH):
        p, a_low = _proj(l, x2, mod, norm_gains, w_main, w_alow)
        o_gla = _gla(l, p, a_low, wgu, bg, gn, tri)
        o_sb = _sb(p, tt)
        mixed = _mix(o_gla, o_sb, w_gla_o_b, w_sb_o_b, p)
        x2 = _out(l, mixed, w_out_b, x2, mod, norm_gains)
        if l + 1 < DEPTH:
            x2, w_main, w_gla_o_b, w_sb_o_b, w_out_b, w_ff1_b, w_ff2_b = _ffn(
                l, x2, mod, norm_gains, w_ff1_b, w_ff2_b, next_f32=(w_in_t,) + plain_f32)
        else:
            x2 = _ffn(l, x2, mod, norm_gains, w_ff1_b, w_ff2_b)
    return x2.reshape(BATCH, SEQ, D_MODEL)
```

```python
import numpy as np
import jax
import jax.numpy as jnp
from jax import lax
from jax.experimental import pallas as pl
from jax.experimental.pallas import tpu as pltpu

F32 = jnp.float32
BF16 = jnp.bfloat16

D_MODEL = 2048
BATCH = 4
SEQ = 2048
DEPTH = 4
ROWS = BATCH * SEQ

CHUNK = 64
GLA_HEADS = 4
GLA_DK = 256
GLA_DV = 512
GLA_K = GLA_HEADS * GLA_DK
GLA_V = GLA_HEADS * GLA_DV
GLA_GATE_RANK = 16
GLA_GATE_TAU = 16.0
SB_HEADS = 16
SB_DH = 128
SB_W = SB_HEADS * SB_DH
D_FF = 4 * D_MODEL
EPS = 1e-6

LOG2E = 1.4426950408889634
LANES = 128
RANK_PAD = LANES
A_LOW_START = 2 * GLA_K + 2 * GLA_V
A_LOW_END = A_LOW_START + GLA_GATE_RANK

P_COLS = 2 * GLA_K + 2 * GLA_V + 3 * SB_W + 2 * D_MODEL
COL_QA = 0
COL_KA = GLA_K
COL_VA = 2 * GLA_K
COL_RA = 2 * GLA_K + GLA_V
COL_QB = 2 * GLA_K + 2 * GLA_V
COL_KB = COL_QB + SB_W
COL_VB = COL_KB + SB_W
COL_GA = COL_VB + SB_W
COL_GB = COL_GA + D_MODEL

VMEM_LIMIT = 56 * 1024 * 1024


def _cparams(sem):
    return pltpu.CompilerParams(dimension_semantics=sem, vmem_limit_bytes=VMEM_LIMIT)


def _softplus(z):
    return jnp.maximum(z, 0.0) + jnp.log(1.0 + jnp.exp(-jnp.abs(z)))


def _split_bf16(a):
    hi = a.astype(BF16)
    lo = (a - hi.astype(F32)).astype(BF16)
    return hi, lo


ADA_TK = 256


def _ada_kernel(c_ref, w_ref, b_ref, o_ref):
    @pl.when(pl.program_id(1) == 0)
    def _():
        o_ref[...] = jnp.broadcast_to(b_ref[...], o_ref.shape)

    c = c_ref[...]
    ca = (c * jax.nn.sigmoid(c)).astype(BF16)
    o_ref[...] += jnp.dot(ca, w_ref[...].astype(BF16), preferred_element_type=F32)


def _ada(c_pad, w_ada, b_ada3):
    n = 6 * D_MODEL
    return pl.pallas_call(
        _ada_kernel,
        out_shape=jax.ShapeDtypeStruct((DEPTH, 8, n), F32),
        grid=(DEPTH, D_MODEL // ADA_TK),
        in_specs=[
            pl.BlockSpec((8, ADA_TK), lambda l, k: (0, k)),
            pl.BlockSpec((None, ADA_TK, n), lambda l, k: (l, k, 0)),
            pl.BlockSpec((None, 1, n), lambda l, k: (l, 0, 0)),
        ],
        out_specs=pl.BlockSpec((None, 8, n), lambda l, k: (l, 0, 0)),
        compiler_params=_cparams(("arbitrary", "arbitrary")),
        name="ada",
    )(c_pad, w_ada, b_ada3)


PROJ_TM = 1024
PROJ_TN = 2048
PACK_TN = 1024
PACK_LO_TILES = A_LOW_START // PACK_TN


def _modulated_norm(x, gain, shift, scale):
    ms = jnp.mean(x * x, axis=-1, keepdims=True)
    y = x * lax.rsqrt(ms + EPS) * gain
    return y * (1.0 + scale) + shift


def _packed_rows(tile, w, nxt, rows, lo_tiles):
    shifted = jnp.concatenate([w[GLA_GATE_RANK:rows, :], nxt], axis=0)
    return jnp.where(tile < lo_tiles, w, shifted).astype(BF16)


def _pack_specs(l, rows, tile_of):
    per = rows // GLA_GATE_RANK
    return [pl.BlockSpec((None, rows, D_MODEL), lambda *g: (l, tile_of(*g), 0)),
            pl.BlockSpec((None, GLA_GATE_RANK, D_MODEL),
                         lambda *g: (l, (tile_of(*g) + 1) * per, 0))]


def _pack_kernel(w_ref, nxt_ref, o_ref):
    o_ref[...] = _packed_rows(pl.program_id(0), w_ref[...], nxt_ref[...], PACK_TN, PACK_LO_TILES)


def _pack_w_in(l, w_in_t):
    tn = PACK_TN
    return pl.pallas_call(
        _pack_kernel,
        out_shape=jax.ShapeDtypeStruct((P_COLS, D_MODEL), BF16),
        grid=(P_COLS // tn,),
        in_specs=_pack_specs(l, tn, lambda j: j),
        out_specs=pl.BlockSpec((tn, D_MODEL), lambda j: (j, 0)),
        compiler_params=_cparams(("arbitrary",)),
        name="pack_w_in",
    )(w_in_t, w_in_t)


_NT = (((1,), (1,)), ((), ()))


def _proj_kernel(x_ref, mod_ref, ng_ref, wt_ref, wat_ref, p_ref, al_ref, h_ref):
    @pl.when(pl.program_id(1) == 0)
    def _():
        h = _modulated_norm(x_ref[...], ng_ref[0:1, :], mod_ref[0:1, :], mod_ref[1:2, :])
        hb = h.astype(BF16)
        h_ref[...] = hb
        al_ref[...] = lax.dot_general(hb, wat_ref[...], _NT, preferred_element_type=F32)

    p_ref[...] = lax.dot_general(h_ref[...], wt_ref[...], _NT,
                                 preferred_element_type=F32).astype(BF16)


def _mod_spec(l, tm):
    return pl.BlockSpec((None, None, 6, D_MODEL), lambda i, *_: (l, i * tm // SEQ, 0, 0))


def _gain_spec(l):
    return pl.BlockSpec((None, 4, D_MODEL), lambda *_: (l, 0, 0))


def _proj(l, x2, mod, norm_gains, w_main, w_alow):
    tm, tn = PROJ_TM, PROJ_TN
    return pl.pallas_call(
        _proj_kernel,
        out_shape=(jax.ShapeDtypeStruct((ROWS, P_COLS), BF16),
                   jax.ShapeDtypeStruct((ROWS, RANK_PAD), F32)),
        grid=(ROWS // tm, P_COLS // tn),
        in_specs=[
            pl.BlockSpec((tm, D_MODEL), lambda i, j: (i, 0)),
            _mod_spec(l, tm),
            _gain_spec(l),
            pl.BlockSpec((tn, D_MODEL), lambda i, j: (j, 0)),
            pl.BlockSpec((None, RANK_PAD, D_MODEL), lambda i, j: (l, 0, 0)),
        ],
        out_specs=(pl.BlockSpec((tm, tn), lambda i, j: (i, j)),
                   pl.BlockSpec((tm, RANK_PAD), lambda i, j: (i, 0))),
        scratch_shapes=[pltpu.VMEM((tm, D_MODEL), BF16)],
        compiler_params=_cparams(("arbitrary", "arbitrary")),
        name="proj",
    )(x2, mod, norm_gains, w_main, w_alow)


GLA_T = 512
GLA_HB = 4


def _gla_kernel(q_ref, k_ref, v_ref, r_ref, al_ref, wgu_ref, bg_ref, gn_ref, tri_ref,
                o_ref, st_ref):
    @pl.when(pl.program_id(2) == 0)
    def _():
        st_ref[...] = jnp.zeros_like(st_ref)

    a = jnp.dot(al_ref[...].astype(BF16), wgu_ref[...], preferred_element_type=F32) + bg_ref[...]
    log_a = -_softplus(-a) * (1.0 / GLA_GATE_TAU)
    tri = tri_ref[...]

    n_chunks = GLA_T // CHUNK

    def chunk_rows(j):
        return slice(j * CHUNK, (j + 1) * CHUNK)

    def chunk_kv(j, h):
        rows = chunk_rows(j)
        kcols = slice(h * GLA_DK, (h + 1) * GLA_DK)
        hi, lo = _split_bf16(log_a[rows, kcols])
        g = (jnp.dot(tri, hi, preferred_element_type=F32)
             + jnp.dot(tri, lo, preferred_element_type=F32))
        g_tot = g[CHUNK - 1:CHUNK, :]
        kd = (k_ref[rows, kcols].astype(F32) * jnp.exp(g_tot - g)).astype(BF16)
        kv_t = lax.dot_general(v_ref[rows, h * GLA_DV:(h + 1) * GLA_DV], kd,
                               (((0,), (0,)), ((), ())), preferred_element_type=F32)
        return jnp.exp(g_tot), kv_t

    ahead = 2
    pending = [[chunk_kv(j, h) for j in range(ahead)] for h in range(GLA_HB)]
    for j in range(n_chunks):
        rows = chunk_rows(j)
        for h in range(GLA_HB):
            vcols = slice(h * GLA_DV, (h + 1) * GLA_DV)
            decay, kv_t = pending[h].pop(0)
            if j + ahead < n_chunks:
                pending[h].append(chunk_kv(j + ahead, h))
            st = st_ref[h] * decay + kv_t
            st_ref[h] = st
            o = lax.dot_general(q_ref[rows, h * GLA_DK:(h + 1) * GLA_DK], st.astype(BF16), _NT,
                                preferred_element_type=F32) * (GLA_DK ** -0.5)
            ms = jnp.mean(o * o, axis=-1, keepdims=True)
            on = o * lax.rsqrt(ms + EPS) * gn_ref[h]
            r = r_ref[rows, vcols].astype(F32)
            o_ref[rows, vcols] = (on * (r * jax.nn.sigmoid(r))).astype(BF16)


def _gla(l, p, a_low, wgu, bg, gn, tri):
    t, hb = GLA_T, GLA_HB
    nt = SEQ // t
    wk, wv = hb * GLA_DK, hb * GLA_DV

    def row(b, h, i):
        return b * nt + i

    return pl.pallas_call(
        _gla_kernel,
        out_shape=jax.ShapeDtypeStruct((ROWS, GLA_V), BF16),
        grid=(BATCH, GLA_HEADS // hb, nt),
        in_specs=[
            pl.BlockSpec((t, wk), lambda b, h, i: (row(b, h, i), COL_QA // wk + h)),
            pl.BlockSpec((t, wk), lambda b, h, i: (row(b, h, i), COL_KA // wk + h)),
            pl.BlockSpec((t, wv), lambda b, h, i: (row(b, h, i), COL_VA // wv + h)),
            pl.BlockSpec((t, wv), lambda b, h, i: (row(b, h, i), COL_RA // wv + h)),
            pl.BlockSpec((t, RANK_PAD), lambda b, h, i: (row(b, h, i), 0)),
            pl.BlockSpec((None, RANK_PAD, wk), lambda b, h, i: (l, 0, h)),
            pl.BlockSpec((None, 1, wk), lambda b, h, i: (l, 0, h)),
            pl.BlockSpec((None, hb, 1, GLA_DV), lambda b, h, i: (l, h, 0, 0)),
            pl.BlockSpec((CHUNK, CHUNK), lambda b, h, i: (0, 0)),
        ],
        out_specs=pl.BlockSpec((t, wv), lambda b, h, i: (row(b, h, i), h)),
        scratch_shapes=[pltpu.VMEM((hb, GLA_DV, GLA_DK), F32)],
        compiler_params=_cparams(("arbitrary", "arbitrary", "arbitrary")),
        name="gla",
    )(p, p, p, p, a_low, wgu, bg, gn, tri)


SB_TQ = 256
SB_TK = 256
SB_HB = 8
SB_HW = SB_HB * SB_DH
SB_MASKED = -1e30


def _sb_kernel(q_ref, k_ref, v_ref, tt_ref, o_ref, acc_ref, carry_ref, z_ref, sp_ref, lb_ref):
    qi = pl.program_id(2)
    tt = tt_ref[...]
    scale = SB_DH ** -0.5 * LOG2E
    acc_ref[...] = jnp.zeros_like(acc_ref)
    carry_ref[...] = jnp.zeros_like(carry_ref)

    heads = [slice(h * SB_DH, (h + 1) * SB_DH) for h in range(SB_HB)]

    def keys(kb):
        return pl.ds(pl.multiple_of(kb * SB_TK, SB_TK), SB_TK)

    def step(w_kb=None, w_slot=None, s_slot=None, past=None, z_kb=None, z_slot=None):
        cs = [None] * SB_HB

        def matmuls(h):
            if w_kb is not None:
                cs[h] = jnp.dot(sp_ref[w_slot, h], tt, preferred_element_type=F32)
            if z_kb is not None:
                z_ref[z_slot, h] = lax.dot_general(q_ref[:, heads[h]], k_ref[keys(z_kb), heads[h]],
                                                   _NT, preferred_element_type=F32) * scale

        matmuls(0)
        for h in range(SB_HB):
            if h + 1 < SB_HB:
                matmuls(h + 1)
            if s_slot is not None:
                z = z_ref[s_slot, h]
                sp = jnp.maximum(z, 0.0) + jnp.log2(1.0 + jnp.exp2(-jnp.abs(z)))
                log_beta = z - sp
                if past is not None:
                    sp = jnp.where(past, sp, 0.0)
                    log_beta = jnp.where(past, log_beta, SB_MASKED)
                sp_ref[s_slot, h] = sp.astype(BF16)
                lb_ref[s_slot, h] = log_beta
            if w_kb is not None:
                carry = carry_ref[h]
                later = cs[h][:, :SB_TK] + jnp.concatenate([carry] * (SB_TK // LANES), axis=1)
                w = jnp.exp2(lb_ref[w_slot, h] - later).astype(BF16)
                carry_ref[h] = carry + cs[h][:, SB_TK:]
                acc_ref[h] += jnp.dot(w, v_ref[keys(w_kb), heads[h]],
                                      preferred_element_type=F32)

    def block(j):
        return jnp.maximum(qi - j, 0)

    t_idx = lax.broadcasted_iota(jnp.int32, (SB_TQ, SB_TK), 0)
    s_idx = lax.broadcasted_iota(jnp.int32, (SB_TQ, SB_TK), 1)
    step(z_kb=qi, z_slot=0)
    step(s_slot=0, past=s_idx < t_idx, z_kb=block(1), z_slot=1)

    def pair(i, _):
        j = 2 * i
        step(w_kb=qi - j, w_slot=0, s_slot=1, z_kb=block(j + 2), z_slot=0)
        step(w_kb=qi - j - 1, w_slot=1, s_slot=0, z_kb=block(j + 3), z_slot=1)
        return 0

    lax.fori_loop(0, qi // 2, pair, 0)

    @pl.when(qi % 2 == 1)
    def _():
        step(w_kb=1, w_slot=0, s_slot=1)
        step(w_kb=0, w_slot=1)

    @pl.when(qi % 2 == 0)
    def _():
        step(w_kb=0, w_slot=0)

    for h in range(SB_HB):
        o_ref[:, h * SB_DH:(h + 1) * SB_DH] = acc_ref[h].astype(BF16)


def _sb(p, tt):
    nq = SEQ // SB_TQ
    return pl.pallas_call(
        _sb_kernel,
        out_shape=jax.ShapeDtypeStruct((ROWS, SB_W), BF16),
        grid=(BATCH, SB_HEADS // SB_HB, nq),
        in_specs=[
            pl.BlockSpec((SB_TQ, SB_HW), lambda b, h, i: (b * nq + i, COL_QB // SB_HW + h)),
            pl.BlockSpec((SEQ, SB_HW), lambda b, h, i: (b, COL_KB // SB_HW + h)),
            pl.BlockSpec((SEQ, SB_HW), lambda b, h, i: (b, COL_VB // SB_HW + h)),
            pl.BlockSpec((SB_TK, SB_TK + LANES), lambda b, h, i: (0, 0)),
        ],
        out_specs=pl.BlockSpec((SB_TQ, SB_HW), lambda b, h, i: (b * nq + i, h)),
        scratch_shapes=[pltpu.VMEM((SB_HB, SB_TQ, SB_DH), F32),
                        pltpu.VMEM((SB_HB, SB_TQ, LANES), F32),
                        pltpu.VMEM((2, SB_HB, SB_TQ, SB_TK), F32),
                        pltpu.VMEM((2, SB_HB, SB_TQ, SB_TK), BF16),
                        pltpu.VMEM((2, SB_HB, SB_TQ, SB_TK), F32)],
        compiler_params=_cparams(("arbitrary", "arbitrary", "arbitrary")),
        name="sb",
    )(p, p, p, tt)


MIX_TM = 1024
MIX_TN = 1024


def _mix_kernel(oa_ref, ob_ref, wa_ref, wb_ref, ga_ref, gb_ref, o_ref):
    ga = jax.nn.sigmoid(ga_ref[...].astype(F32))
    gb = jax.nn.sigmoid(gb_ref[...].astype(F32))
    ya = jnp.dot(oa_ref[...], wa_ref[...], preferred_element_type=F32)
    yb = jnp.dot(ob_ref[...], wb_ref[...], preferred_element_type=F32)
    o_ref[...] = (ga * ya + gb * yb).astype(BF16)


def _mix(o_gla, o_sb, w_gla_o, w_sb_o, p):
    tm, tn = MIX_TM, MIX_TN
    return pl.pallas_call(
        _mix_kernel,
        out_shape=jax.ShapeDtypeStruct((ROWS, D_MODEL), BF16),
        grid=(ROWS // tm, D_MODEL // tn),
        in_specs=[
            pl.BlockSpec((tm, GLA_V), lambda i, j: (i, 0)),
            pl.BlockSpec((tm, SB_W), lambda i, j: (i, 0)),
            pl.BlockSpec((GLA_V, tn), lambda i, j: (0, j)),
            pl.BlockSpec((SB_W, tn), lambda i, j: (0, j)),
            pl.BlockSpec((tm, tn), lambda i, j: (i, COL_GA // tn + j)),
            pl.BlockSpec((tm, tn), lambda i, j: (i, COL_GB // tn + j)),
        ],
        out_specs=pl.BlockSpec((tm, tn), lambda i, j: (i, j)),
        compiler_params=_cparams(("arbitrary", "arbitrary")),
        name="mix",
    )(o_gla, o_sb, w_gla_o, w_sb_o, p, p)


OUT_TM = 512


def _out_kernel(m_ref, w_ref, x_ref, mod_ref, ng_ref, o_ref):
    y = jnp.dot(m_ref[...], w_ref[...], preferred_element_type=F32)
    ms = jnp.mean(y * y, axis=-1, keepdims=True)
    yn = y * lax.rsqrt(ms + EPS) * ng_ref[1:2, :]
    o_ref[...] = x_ref[...] + mod_ref[2:3, :] * yn


def _out(l, mixed, w_out, x2, mod, norm_gains):
    tm = OUT_TM
    return pl.pallas_call(
        _out_kernel,
        out_shape=jax.ShapeDtypeStruct((ROWS, D_MODEL), F32),
        grid=(ROWS // tm,),
        in_specs=[
            pl.BlockSpec((tm, D_MODEL), lambda i: (i, 0)),
            pl.BlockSpec((D_MODEL, D_MODEL), lambda i: (0, 0)),
            pl.BlockSpec((tm, D_MODEL), lambda i: (i, 0)),
            _mod_spec(l, tm),
            _gain_spec(l),
        ],
        out_specs=pl.BlockSpec((tm, D_MODEL), lambda i: (i, 0)),
        compiler_params=_cparams(("arbitrary",)),
        name="out",
    )(mixed, w_out, x2, mod, norm_gains)


FFN_TM = 512
FFN_TF = 1024
FFN_STEPS = (ROWS // FFN_TM) * (D_FF // FFN_TF)
CAST_IN_ROWS = P_COLS // FFN_STEPS
CAST_IN_LO_TILES = A_LOW_START // CAST_IN_ROWS
N_PLAIN_CASTS = 5


def _ffn_cast_kernel(x_ref, mod_ref, ng_ref, w1_ref, w2_ref, win_ref, winx_ref, *rest):
    plain_in = rest[:N_PLAIN_CASTS]
    o_ref, win_o = rest[N_PLAIN_CASTS:N_PLAIN_CASTS + 2]
    plain_out = rest[N_PLAIN_CASTS + 2:2 * N_PLAIN_CASTS + 2]
    h_ref, acc_ref = rest[2 * N_PLAIN_CASTS + 2:]

    def cast_slabs():
        step = pl.program_id(0) * pl.num_programs(1) + pl.program_id(1)
        win_o[...] = _packed_rows(step, win_ref[...], winx_ref[...], CAST_IN_ROWS, CAST_IN_LO_TILES)
        for src, dst in zip(plain_in, plain_out):
            dst[...] = src[...].astype(BF16)

    _ffn_kernel(x_ref, mod_ref, ng_ref, w1_ref, w2_ref, o_ref, h_ref, acc_ref, beside_dots=cast_slabs)


def _ffn_kernel(x_ref, mod_ref, ng_ref, w1_ref, w2_ref, o_ref, h_ref, acc_ref, beside_dots=None):
    f = pl.program_id(1)

    @pl.when(f == 0)
    def _():
        h = _modulated_norm(x_ref[...], ng_ref[2:3, :], mod_ref[3:4, :], mod_ref[4:5, :])
        h_ref[...] = h.astype(BF16)
        acc_ref[...] = jnp.zeros_like(acc_ref)

    if beside_dots is not None:
        beside_dots()
    a = jnp.dot(h_ref[...], w1_ref[...], preferred_element_type=F32)
    a = jnp.square(jnp.maximum(a, 0.0)).astype(BF16)
    acc_ref[...] += jnp.dot(a, w2_ref[...], preferred_element_type=F32)

    @pl.when(f == pl.num_programs(1) - 1)
    def _():
        y = acc_ref[...]
        ms = jnp.mean(y * y, axis=-1, keepdims=True)
        yn = y * lax.rsqrt(ms + EPS) * ng_ref[3:4, :]
        o_ref[...] = x_ref[...] + mod_ref[5:6, :] * yn


def _ffn(l, x2, mod, norm_gains, w1, w2, next_f32=None):
    tm, tf = FFN_TM, FFN_TF
    nf = D_FF // tf
    in_specs = [
        pl.BlockSpec((tm, D_MODEL), lambda i, f: (i, 0)),
        _mod_spec(l, tm),
        _gain_spec(l),
        pl.BlockSpec((D_MODEL, tf), lambda i, f: (0, f)),
        pl.BlockSpec((tf, D_MODEL), lambda i, f: (f, 0)),
    ]
    x_spec = pl.BlockSpec((tm, D_MODEL), lambda i, f: (i, 0))
    x_shape = jax.ShapeDtypeStruct((ROWS, D_MODEL), F32)
    scratch = [pltpu.VMEM((tm, D_MODEL), BF16), pltpu.VMEM((tm, D_MODEL), F32)]
    params = _cparams(("arbitrary", "arbitrary"))
    if next_f32 is None:
        return pl.pallas_call(
            _ffn_kernel, out_shape=x_shape, grid=(ROWS // tm, nf), in_specs=in_specs,
            out_specs=x_spec, scratch_shapes=scratch, compiler_params=params, name="ffn",
        )(x2, mod, norm_gains, w1, w2)

    def step(i, f):
        return i * nf + f

    def slab(rows, cols):
        return pl.BlockSpec((rows, cols), lambda i, f: (step(i, f), 0))

    def slab_of_next(rows, cols):
        return pl.BlockSpec((None, rows, cols), lambda i, f: (l + 1, step(i, f), 0))

    w_in_t, plain = next_f32[0], next_f32[1:]
    assert len(plain) == N_PLAIN_CASTS
    shapes = [w.shape[1:] for w in plain]
    slabs = [(r // FFN_STEPS, c) for r, c in shapes]
    return pl.pallas_call(
        _ffn_cast_kernel,
        out_shape=(x_shape, jax.ShapeDtypeStruct((P_COLS, D_MODEL), BF16),
                   *[jax.ShapeDtypeStruct(s, BF16) for s in shapes]),
        grid=(ROWS // tm, nf),
        in_specs=(in_specs + _pack_specs(l + 1, CAST_IN_ROWS, step)
                  + [slab_of_next(r, c) for r, c in slabs]),
        out_specs=(x_spec, slab(CAST_IN_ROWS, D_MODEL), *[slab(r, c) for r, c in slabs]),
        scratch_shapes=scratch, compiler_params=params, name="ffn",
    )(x2, mod, norm_gains, w1, w2, w_in_t, w_in_t, *plain)


def _cumsum_constants():
    j = np.arange(CHUNK)
    tri = (j[:, None] >= j[None, :]).astype(np.float32)
    jj = np.arange(SB_TK)
    s = np.arange(SB_TK + LANES)
    tt =((jj[:, None] > s[None, :]) | (s[None, :] >= SB_TK)).astype(np.float32)
    return jnp.asarray(tri, BF16), jnp.asarray(tt, BF16)


def kernel(x, c, w_ada, b_ada, norm_gains, w_in, w_gate_up, b_gate, gla_norm_gain,
           w_gla_o, w_sb_o, w_out, w_ff1, w_ff2):
    tri, tt = _cumsum_constants()

    c_pad = jnp.pad(c, ((0, 8 - BATCH), (0, 0)))
    mod = _ada(c_pad, w_ada, b_ada.reshape(DEPTH, 1, 6 * D_MODEL))
    mod = mod[:, :BATCH, :].reshape(DEPTH, BATCH, 6, D_MODEL)

    w_in_t = jnp.swapaxes(w_in, 1, 2)
    w_alow = jnp.pad(w_in_t[:, A_LOW_START:A_LOW_END, :],
                     ((0, 0), (0, RANK_PAD - GLA_GATE_RANK), (0, 0))).astype(BF16)
    wgu = jnp.pad(w_gate_up, ((0, 0), (0, RANK_PAD - GLA_GATE_RANK), (0, 0))).astype(BF16)
    bg = b_gate.reshape(DEPTH, 1, GLA_K)
    gn = gla_norm_gain.reshape(DEPTH, GLA_HEADS, 1, GLA_DV)
    plain_f32 = (w_gla_o, w_sb_o, w_out, w_ff1, w_ff2)
    w_main = _pack_w_in(0, w_in_t)
    w_gla_o_b, w_sb_o_b, w_out_b, w_ff1_b, w_ff2_b = (w[0].astype(BF16) for w in plain_f32)

    x2 = x.reshape(ROWS, D_MODEL)
    for l in range(DEPTH):
        p, a_low = _proj(l, x2, mod, norm_gains, w_main, w_alow)
        o_gla = _gla(l, p, a_low, wgu, bg, gn, tri)
        o_sb = _sb(p, tt)
        mixed = _mix(o_gla, o_sb, w_gla_o_b, w_sb_o_b, p)
        x2 = _out(l, mixed, w_out_b, x2, mod, norm_gains)
        if l + 1 < DEPTH:
            x2, w_main, w_gla_o_b, w_sb_o_b, w_out_b, w_ff1_b, w_ff2_b = _ffn(
                l, x2, mod, norm_gains, w_ff1_b, w_ff2_b, next_f32=(w_in_t,) + plain_f32)
        else:
            x2 = _ffn(l, x2, mod, norm_gains, w_ff1_b, w_ff2_b)
    return x2.reshape(BATCH, SEQ, D_MODEL)
```

```python
import numpy as np
import jax
import jax.numpy as jnp
from jax import lax
from jax.experimental import pallas as pl
from jax.experimental.pallas import tpu as pltpu

F32 = jnp.float32
BF16 = jnp.bfloat16

D_MODEL = 2048
BATCH = 4
SEQ = 2048
DEPTH = 4
ROWS = BATCH * SEQ

CHUNK = 64
GLA_HEADS = 4
GLA_DK = 256
GLA_DV = 512
GLA_K = GLA_HEADS * GLA_DK
GLA_V = GLA_HEADS * GLA_DV
GLA_GATE_RANK = 16
GLA_GATE_TAU = 16.0
SB_HEADS = 16
SB_DH = 128
SB_W = SB_HEADS * SB_DH
D_FF = 4 * D_MODEL
EPS = 1e-6

LOG2E = 1.4426950408889634
LANES = 128
RANK_PAD = LANES
A_LOW_START = 2 * GLA_K + 2 * GLA_V
A_LOW_END = A_LOW_START + GLA_GATE_RANK

P_COLS = 2 * GLA_K + 2 * GLA_V + 3 * SB_W + 2 * D_MODEL
COL_QA = 0
COL_KA = GLA_K
COL_VA = 2 * GLA_K
COL_RA = 2 * GLA_K + GLA_V
COL_QB = 2 * GLA_K + 2 * GLA_V
COL_KB = COL_QB + SB_W
COL_VB = COL_KB + SB_W
COL_GA = COL_VB + SB_W
COL_GB = COL_GA + D_MODEL

VMEM_LIMIT = 56 * 1024 * 1024


def _cparams(sem):
    return pltpu.CompilerParams(dimension_semantics=sem, vmem_limit_bytes=VMEM_LIMIT)


def _softplus(z):
    return jnp.maximum(z, 0.0) + jnp.log(1.0 + jnp.exp(-jnp.abs(z)))


def _split_bf16(a):
    hi = a.astype(BF16)
    lo = (a - hi.astype(F32)).astype(BF16)
    return hi, lo


ADA_TK = 256


def _ada_kernel(c_ref, w_ref, b_ref, o_ref):
    @pl.when(pl.program_id(1) == 0)
    def _():
        o_ref[...] = jnp.broadcast_to(b_ref[...], o_ref.shape)

    c = c_ref[...]
    ca = (c * jax.nn.sigmoid(c)).astype(BF16)
    o_ref[...] += jnp.dot(ca, w_ref[...].astype(BF16), preferred_element_type=F32)


def _ada(c_pad, w_ada, b_ada3):
    n = 6 * D_MODEL
    return pl.pallas_call(
        _ada_kernel,
        out_shape=jax.ShapeDtypeStruct((DEPTH, 8, n), F32),
        grid=(DEPTH, D_MODEL // ADA_TK),
        in_specs=[
            pl.BlockSpec((8, ADA_TK), lambda l, k: (0, k)),
            pl.BlockSpec((None, ADA_TK, n), lambda l, k: (l, k, 0)),
            pl.BlockSpec((None, 1, n), lambda l, k: (l, 0, 0)),
        ],
        out_specs=pl.BlockSpec((None, 8, n), lambda l, k: (l, 0, 0)),
        compiler_params=_cparams(("arbitrary", "arbitrary")),
        name="ada",
    )(c_pad, w_ada, b_ada3)


PROJ_TM = 1024
PROJ_TN = 2048
PACK_TN = 1024
PACK_LO_TILES = A_LOW_START // PACK_TN


ROW_CHUNK = 16
ROW_UNROLL = 8


def _for_row_chunks(n_rows, body):
    def step(i, carry):
        body(pl.ds(pl.multiple_of(i * ROW_CHUNK, ROW_CHUNK), ROW_CHUNK))
        return carry

    lax.fori_loop(0, n_rows // ROW_CHUNK, step, 0, unroll=ROW_UNROLL)


def _chunk_rows(*row_vectors):
    return [jnp.broadcast_to(v, (ROW_CHUNK, v.shape[-1])) for v in row_vectors]


def _modulated_norm(x, gain, shift, scale):
    ms = jnp.mean(x * x, axis=-1, keepdims=True)
    y = x * lax.rsqrt(ms + EPS) * gain
    return y * (1.0 + scale) + shift


def _gated_norm_residual(x, y, gain, gate):
    ms = jnp.mean(y * y, axis=-1, keepdims=True)
    return x + gate * (y * lax.rsqrt(ms + EPS) * gain)


def _packed_rows(tile, w, nxt, rows, lo_tiles):
    shifted = jnp.concatenate([w[GLA_GATE_RANK:rows, :], nxt], axis=0)
    return jnp.where(tile < lo_tiles, w, shifted).astype(BF16)


def _pack_specs(l, rows, tile_of):
    per = rows // GLA_GATE_RANK
    return [pl.BlockSpec((None, rows, D_MODEL), lambda *g: (l, tile_of(*g), 0)),
            pl.BlockSpec((None, GLA_GATE_RANK, D_MODEL),
                         lambda *g: (l, (tile_of(*g) + 1) * per, 0))]


def _pack_kernel(w_ref, nxt_ref, o_ref):
    o_ref[...] = _packed_rows(pl.program_id(0), w_ref[...], nxt_ref[...], PACK_TN, PACK_LO_TILES)


def _pack_w_in(l, w_in_t):
    tn = PACK_TN
    return pl.pallas_call(
        _pack_kernel,
        out_shape=jax.ShapeDtypeStruct((P_COLS, D_MODEL), BF16),
        grid=(P_COLS // tn,),
        in_specs=_pack_specs(l, tn, lambda j: j),
        out_specs=pl.BlockSpec((tn, D_MODEL), lambda j: (j, 0)),
        compiler_params=_cparams(("arbitrary",)),
        name="pack_w_in",
    )(w_in_t, w_in_t)


_NT = (((1,), (1,)), ((), ()))


def _proj_kernel(x_ref, mod_ref, ng_ref, wt_ref, wat_ref, p_ref, al_ref, h_ref):
    @pl.when(pl.program_id(1) == 0)
    def _():
        h = _modulated_norm(x_ref[...], ng_ref[0:1, :], mod_ref[0:1, :], mod_ref[1:2, :])
        hb = h.astype(BF16)
        h_ref[...] = hb
        al_ref[...] = lax.dot_general(hb, wat_ref[...], _NT, preferred_element_type=F32)

    p_ref[...] = lax.dot_general(h_ref[...], wt_ref[...], _NT,
                                 preferred_element_type=F32).astype(BF16)


def _mod_spec(l, tm):
    return pl.BlockSpec((None, None, 6, D_MODEL), lambda i, *_: (l, i * tm // SEQ, 0, 0))


def _gain_spec(l):
    return pl.BlockSpec((None, 4, D_MODEL), lambda *_: (l, 0, 0))


def _proj(l, x2, mod, norm_gains, w_main, w_alow):
    tm, tn = PROJ_TM, PROJ_TN
    return pl.pallas_call(
        _proj_kernel,
        out_shape=(jax.ShapeDtypeStruct((ROWS, P_COLS), BF16),
                   jax.ShapeDtypeStruct((ROWS, RANK_PAD), F32)),
        grid=(ROWS // tm, P_COLS // tn),
        in_specs=[
            pl.BlockSpec((tm, D_MODEL), lambda i, j: (i, 0)),
            _mod_spec(l, tm),
            _gain_spec(l),
            pl.BlockSpec((tn, D_MODEL), lambda i, j: (j, 0)),
            pl.BlockSpec((None, RANK_PAD, D_MODEL), lambda i, j: (l, 0, 0)),
        ],
        out_specs=(pl.BlockSpec((tm, tn), lambda i, j: (i, j)),
                   pl.BlockSpec((tm, RANK_PAD), lambda i, j: (i, 0))),
        scratch_shapes=[pltpu.VMEM((tm, D_MODEL), BF16)],
        compiler_params=_cparams(("arbitrary", "arbitrary")),
        name="proj",
    )(x2, mod, norm_gains, w_main, w_alow)


GLA_T = 512
GLA_HB = 4


def _gla_kernel(q_ref, k_ref, v_ref, r_ref, al_ref, wgu_ref, bg_ref, gn_ref, tri_ref,
                o_ref, st_ref):
    @pl.when(pl.program_id(2) == 0)
    def _():
        st_ref[...] = jnp.zeros_like(st_ref)

    a = jnp.dot(al_ref[...].astype(BF16), wgu_ref[...], preferred_element_type=F32) + bg_ref[...]
    log_a = -_softplus(-a) * (1.0 / GLA_GATE_TAU)
    tri = tri_ref[...]

    n_chunks = GLA_T // CHUNK

    def chunk_rows(j):
        return slice(j * CHUNK, (j + 1) * CHUNK)

    def chunk_kv(j, h):
        rows = chunk_rows(j)
        kcols = slice(h * GLA_DK, (h + 1) * GLA_DK)
        hi, lo = _split_bf16(log_a[rows, kcols])
        g = (jnp.dot(tri, hi, preferred_element_type=F32)
             + jnp.dot(tri, lo, preferred_element_type=F32))
        g_tot = g[CHUNK - 1:CHUNK, :]
        kd = (k_ref[rows, kcols].astype(F32) * jnp.exp(g_tot - g)).astype(BF16)
        kv_t = lax.dot_general(v_ref[rows, h * GLA_DV:(h + 1) * GLA_DV], kd,
                               (((0,), (0,)), ((), ())), preferred_element_type=F32)
        return jnp.exp(g_tot), kv_t

    ahead = 2
    pending = [[chunk_kv(j, h) for j in range(ahead)] for h in range(GLA_HB)]
    for j in range(n_chunks):
        rows = chunk_rows(j)
        for h in range(GLA_HB):
            vcols = slice(h * GLA_DV, (h + 1) * GLA_DV)
            decay, kv_t = pending[h].pop(0)
            if j + ahead < n_chunks:
                pending[h].append(chunk_kv(j + ahead, h))
            st = st_ref[h] * decay + kv_t
            st_ref[h] = st
            o = lax.dot_general(q_ref[rows, h * GLA_DK:(h + 1) * GLA_DK], st.astype(BF16), _NT,
                                preferred_element_type=F32) * (GLA_DK ** -0.5)
            ms = jnp.mean(o * o, axis=-1, keepdims=True)
            on = o * lax.rsqrt(ms + EPS) * gn_ref[h]
            r = r_ref[rows, vcols].astype(F32)
            o_ref[rows, vcols] = (on * (r * jax.nn.sigmoid(r))).astype(BF16)


def _gla(l, p, a_low, wgu, bg, gn, tri):
    t, hb = GLA_T, GLA_HB
    nt = SEQ // t
    wk, wv = hb * GLA_DK, hb * GLA_DV

    def row(b, h, i):
        return b * nt + i

    return pl.pallas_call(
        _gla_kernel,
        out_shape=jax.ShapeDtypeStruct((ROWS, GLA_V), BF16),
        grid=(BATCH, GLA_HEADS // hb, nt),
        in_specs=[
            pl.BlockSpec((t, wk), lambda b, h, i: (row(b, h, i), COL_QA // wk + h)),
            pl.BlockSpec((t, wk), lambda b, h, i: (row(b, h, i), COL_KA // wk + h)),
            pl.BlockSpec((t, wv), lambda b, h, i: (row(b, h, i), COL_VA // wv + h)),
            pl.BlockSpec((t, wv), lambda b, h, i: (row(b, h, i), COL_RA // wv + h)),
            pl.BlockSpec((t, RANK_PAD), lambda b, h, i: (row(b, h, i), 0)),
            pl.BlockSpec((None, RANK_PAD, wk), lambda b, h, i: (l, 0, h)),
            pl.BlockSpec((None, 1, wk), lambda b, h, i: (l, 0, h)),
            pl.BlockSpec((None, hb, 1, GLA_DV), lambda b, h, i: (l, h, 0, 0)),
            pl.BlockSpec((CHUNK, CHUNK), lambda b, h, i: (0, 0)),
        ],
        out_specs=pl.BlockSpec((t, wv), lambda b, h, i: (row(b, h, i), h)),
        scratch_shapes=[pltpu.VMEM((hb, GLA_DV, GLA_DK), F32)],
        compiler_params=_cparams(("arbitrary", "arbitrary", "arbitrary")),
        name="gla",
    )(p, p, p, p, a_low, wgu, bg, gn, tri)


SB_TQ = 256
SB_TK = 256
SB_HB = 8
SB_HW = SB_HB * SB_DH
SB_MASKED = -1e30


def _sb_kernel(q_ref, k_ref, v_ref, tt_ref, o_ref, acc_ref, carry_ref, z_ref, sp_ref, lb_ref):
    qi = pl.program_id(2)
    tt = tt_ref[...]
    scale = SB_DH ** -0.5 * LOG2E
    acc_ref[...] = jnp.zeros_like(acc_ref)
    carry_ref[...] = jnp.zeros_like(carry_ref)

    heads = [slice(h * SB_DH, (h + 1) * SB_DH) for h in range(SB_HB)]

    def keys(kb):
        return pl.ds(pl.multiple_of(kb * SB_TK, SB_TK), SB_TK)

    def step(w_kb=None, w_slot=None, s_slot=None, past=None, z_kb=None, z_slot=None):
        cs = [None] * SB_HB

        def matmuls(h):
            if w_kb is not None:
                cs[h] = jnp.dot(sp_ref[w_slot, h], tt, preferred_element_type=F32)
            if z_kb is not None:
                z_ref[z_slot, h] = lax.dot_general(q_ref[:, heads[h]], k_ref[keys(z_kb), heads[h]],
                                                   _NT, preferred_element_type=F32) * scale

        matmuls(0)
        for h in range(SB_HB):
            if h + 1 < SB_HB:
                matmuls(h + 1)
            if s_slot is not None:
                z = z_ref[s_slot, h]
                sp = jnp.maximum(z, 0.0) + jnp.log2(1.0 + jnp.exp2(-jnp.abs(z)))
                log_beta = z - sp
                if past is not None:
                    sp = jnp.where(past, sp, 0.0)
                    log_beta = jnp.where(past, log_beta, SB_MASKED)
                sp_ref[s_slot, h] = sp.astype(BF16)
                lb_ref[s_slot, h] = log_beta
            if w_kb is not None:
                carry = carry_ref[h]
                later = cs[h][:, :SB_TK] + jnp.concatenate([carry] * (SB_TK // LANES), axis=1)
                w = jnp.exp2(lb_ref[w_slot, h] - later).astype(BF16)
                carry_ref[h] = carry + cs[h][:, SB_TK:]
                acc_ref[h] += jnp.dot(w, v_ref[keys(w_kb), heads[h]],
                                      preferred_element_type=F32)

    def block(j):
        return jnp.maximum(qi - j, 0)

    t_idx = lax.broadcasted_iota(jnp.int32, (SB_TQ, SB_TK), 0)
    s_idx = lax.broadcasted_iota(jnp.int32, (SB_TQ, SB_TK), 1)
    step(z_kb=qi, z_slot=0)
    step(s_slot=0, past=s_idx < t_idx, z_kb=block(1), z_slot=1)

    def pair(i, _):
        j = 2 * i
        step(w_kb=qi - j, w_slot=0, s_slot=1, z_kb=block(j + 2), z_slot=0)
        step(w_kb=qi - j - 1, w_slot=1, s_slot=0, z_kb=block(j + 3), z_slot=1)
        return 0

    lax.fori_loop(0, qi // 2, pair, 0)

    @pl.when(qi % 2 == 1)
    def _():
        step(w_kb=1, w_slot=0, s_slot=1)
        step(w_kb=0, w_slot=1)

    @pl.when(qi % 2 == 0)
    def _():
        step(w_kb=0, w_slot=0)

    for h in range(SB_HB):
        o_ref[:, h * SB_DH:(h + 1) * SB_DH] = acc_ref[h].astype(BF16)


def _sb(p, tt):
    nq = SEQ // SB_TQ
    return pl.pallas_call(
        _sb_kernel,
        out_shape=jax.ShapeDtypeStruct((ROWS, SB_W), BF16),
        grid=(BATCH, SB_HEADS // SB_HB, nq),
        in_specs=[
            pl.BlockSpec((SB_TQ, SB_HW), lambda b, h, i: (b * nq + i, COL_QB // SB_HW + h)),
            pl.BlockSpec((SEQ, SB_HW), lambda b, h, i: (b, COL_KB // SB_HW + h)),
            pl.BlockSpec((SEQ, SB_HW), lambda b, h, i: (b, COL_VB // SB_HW + h)),
            pl.BlockSpec((SB_TK, SB_TK + LANES), lambda b, h, i: (0, 0)),
        ],
        out_specs=pl.BlockSpec((SB_TQ, SB_HW), lambda b, h, i: (b * nq + i, h)),
        scratch_shapes=[pltpu.VMEM((SB_HB, SB_TQ, SB_DH), F32),
                        pltpu.VMEM((SB_HB, SB_TQ, LANES), F32),
                        pltpu.VMEM((2, SB_HB, SB_TQ, SB_TK), F32),
                        pltpu.VMEM((2, SB_HB, SB_TQ, SB_TK), BF16),
                        pltpu.VMEM((2, SB_HB, SB_TQ, SB_TK), F32)],
        compiler_params=_cparams(("arbitrary", "arbitrary", "arbitrary")),
        name="sb",
    )(p, p, p, tt)


MIX_TM = 1024
MIX_TN = 1024


def _mix_kernel(oa_ref, ob_ref, wa_ref, wb_ref, ga_ref, gb_ref, o_ref):
    ga = jax.nn.sigmoid(ga_ref[...].astype(F32))
    gb = jax.nn.sigmoid(gb_ref[...].astype(F32))
    ya = jnp.dot(oa_ref[...], wa_ref[...], preferred_element_type=F32)
    yb = jnp.dot(ob_ref[...], wb_ref[...], preferred_element_type=F32)
    o_ref[...] = (ga * ya + gb * yb).astype(BF16)


def _mix(o_gla, o_sb, w_gla_o, w_sb_o, p):
    tm, tn = MIX_TM, MIX_TN
    return pl.pallas_call(
        _mix_kernel,
        out_shape=jax.ShapeDtypeStruct((ROWS, D_MODEL), BF16),
        grid=(ROWS // tm, D_MODEL // tn),
        in_specs=[
            pl.BlockSpec((tm, GLA_V), lambda i, j: (i, 0)),
            pl.BlockSpec((tm, SB_W), lambda i, j: (i, 0)),
            pl.BlockSpec((GLA_V, tn), lambda i, j: (0, j)),
            pl.BlockSpec((SB_W, tn), lambda i, j: (0, j)),
            pl.BlockSpec((tm, tn), lambda i, j: (i, COL_GA // tn + j)),
            pl.BlockSpec((tm, tn), lambda i, j: (i, COL_GB // tn + j)),
        ],
        out_specs=pl.BlockSpec((tm, tn), lambda i, j: (i, j)),
        compiler_params=_cparams(("arbitrary", "arbitrary")),
        name="mix",
    )(o_gla, o_sb, w_gla_o, w_sb_o, p, p)


OUT_TM = 512


def _out_kernel(m_ref, w_ref, x_ref, mod_ref, ng_ref, o_ref, y_ref):
    y_ref[...] = jnp.dot(m_ref[...], w_ref[...], preferred_element_type=F32)

    gain, gate = _chunk_rows(ng_ref[1:2, :], mod_ref[2:3, :])

    def finish_rows(rows):
        o_ref[rows, :] = _gated_norm_residual(x_ref[rows, :], y_ref[rows, :], gain, gate)

    _for_row_chunks(OUT_TM, finish_rows)


def _out(l, mixed, w_out, x2, mod, norm_gains):
    tm = OUT_TM
    return pl.pallas_call(
        _out_kernel,
        out_shape=jax.ShapeDtypeStruct((ROWS, D_MODEL), F32),
        grid=(ROWS // tm,),
        in_specs=[
            pl.BlockSpec((tm, D_MODEL), lambda i: (i, 0)),
            pl.BlockSpec((D_MODEL, D_MODEL), lambda i: (0, 0)),
            pl.BlockSpec((tm, D_MODEL), lambda i: (i, 0)),
            _mod_spec(l, tm),
            _gain_spec(l),
        ],
        out_specs=pl.BlockSpec((tm, D_MODEL), lambda i: (i, 0)),
        scratch_shapes=[pltpu.VMEM((tm, D_MODEL), F32)],
        compiler_params=_cparams(("arbitrary",)),
        name="out",
    )(mixed, w_out, x2, mod, norm_gains)


FFN_TM = 512
FFN_TF = 1024
FFN_STEPS = (ROWS // FFN_TM) * (D_FF // FFN_TF)
CAST_IN_ROWS = P_COLS // FFN_STEPS
CAST_IN_LO_TILES = A_LOW_START // CAST_IN_ROWS
N_PLAIN_CASTS = 5


def _ffn_cast_kernel(x_ref, mod_ref, ng_ref, w1_ref, w2_ref, win_ref, winx_ref, *rest):
    plain_in = rest[:N_PLAIN_CASTS]
    o_ref, win_o = rest[N_PLAIN_CASTS:N_PLAIN_CASTS + 2]
    plain_out = rest[N_PLAIN_CASTS + 2:2 * N_PLAIN_CASTS + 2]
    h_ref, acc_ref = rest[2 * N_PLAIN_CASTS + 2:]

    def cast_slabs():
        step = pl.program_id(0) * pl.num_programs(1) + pl.program_id(1)
        win_o[...] = _packed_rows(step, win_ref[...], winx_ref[...], CAST_IN_ROWS, CAST_IN_LO_TILES)
        for src, dst in zip(plain_in, plain_out):
            dst[...] = src[...].astype(BF16)

    _ffn_kernel(x_ref, mod_ref, ng_ref, w1_ref, w2_ref, o_ref, h_ref, acc_ref, beside_dots=cast_slabs)


def _ffn_kernel(x_ref, mod_ref, ng_ref, w1_ref, w2_ref, o_ref, h_ref, acc_ref, beside_dots=None):
    f = pl.program_id(1)

    @pl.when(f == 0)
    def _():
        gain, shift, scale = _chunk_rows(ng_ref[2:3, :], mod_ref[3:4, :], mod_ref[4:5, :])

        def norm_rows(rows):
            h_ref[rows, :] = _modulated_norm(x_ref[rows, :], gain, shift, scale).astype(BF16)

        _for_row_chunks(FFN_TM, norm_rows)
        acc_ref[...] = jnp.zeros_like(acc_ref)

    if beside_dots is not None:
        beside_dots()
    a = jnp.dot(h_ref[...], w1_ref[...], preferred_element_type=F32)
    a = jnp.square(jnp.maximum(a, 0.0)).astype(BF16)
    acc_ref[...] += jnp.dot(a, w2_ref[...], preferred_element_type=F32)

    @pl.when(f == pl.num_programs(1) - 1)
    def _():
        gain, gate = _chunk_rows(ng_ref[3:4, :], mod_ref[5:6, :])

        def finish_rows(rows):
            o_ref[rows, :] = _gated_norm_residual(x_ref[rows, :], acc_ref[rows, :], gain, gate)

        _for_row_chunks(FFN_TM, finish_rows)


def _ffn(l, x2, mod, norm_gains, w1, w2, next_f32=None):
    tm, tf = FFN_TM, FFN_TF
    nf = D_FF // tf
    in_specs = [
        pl.BlockSpec((tm, D_MODEL), lambda i, f: (i, 0)),
        _mod_spec(l, tm),
        _gain_spec(l),
        pl.BlockSpec((D_MODEL, tf), lambda i, f: (0, f)),
        pl.BlockSpec((tf, D_MODEL), lambda i, f: (f, 0)),
    ]
    x_spec = pl.BlockSpec((tm, D_MODEL), lambda i, f: (i, 0))
    x_shape = jax.ShapeDtypeStruct((ROWS, D_MODEL), F32)
    scratch = [pltpu.VMEM((tm, D_MODEL), BF16), pltpu.VMEM((tm, D_MODEL), F32)]
    params = _cparams(("arbitrary", "arbitrary"))
    if next_f32 is None:
        return pl.pallas_call(
            _ffn_kernel, out_shape=x_shape, grid=(ROWS // tm, nf), in_specs=in_specs,
            out_specs=x_spec, scratch_shapes=scratch, compiler_params=params, name="ffn",
        )(x2, mod, norm_gains, w1, w2)

    def step(i, f):
        return i * nf + f

    def slab(rows, cols):
        return pl.BlockSpec((rows, cols), lambda i, f: (step(i, f), 0))

    def slab_of_next(rows, cols):
        return pl.BlockSpec((None, rows, cols), lambda i, f: (l + 1, step(i, f), 0))

    w_in_t, plain = next_f32[0], next_f32[1:]
    assert len(plain) == N_PLAIN_CASTS
    shapes = [w.shape[1:] for w in plain]
    slabs = [(r // FFN_STEPS, c) for r, c in shapes]
    return pl.pallas_call(
        _ffn_cast_kernel,
        out_shape=(x_shape, jax.ShapeDtypeStruct((P_COLS, D_MODEL), BF16),
                   *[jax.ShapeDtypeStruct(s, BF16) for s in shapes]),
        grid=(ROWS // tm, nf),
        in_specs=(in_specs + _pack_specs(l + 1, CAST_IN_ROWS, step)
                  + [slab_of_next(r, c) for r, c in slabs]),
        out_specs=(x_spec, slab(CAST_IN_ROWS, D_MODEL), *[slab(r, c) for r, c in slabs]),
        scratch_shapes=scratch, compiler_params=params, name="ffn",
    )(x2, mod, norm_gains, w1, w2, w_in_t, w_in_t, *plain)


def _cumsum_constants():
    j = np.arange(CHUNK)
    tri = (j[:, None] >= j[None, :]).astype(np.float32)
    jj = np.arange(SB_TK)
    s = np.arange(SB_TK + LANES)
    tt =((jj[:, None] > s[None, :]) | (s[None, :] >= SB_TK)).astype(np.float32)
    return jnp.asarray(tri, BF16), jnp.asarray(tt, BF16)


def kernel(x, c, w_ada, b_ada, norm_gains, w_in, w_gate_up, b_gate, gla_norm_gain,
           w_gla_o, w_sb_o, w_out, w_ff1, w_ff2):
    tri, tt = _cumsum_constants()

    c_pad = jnp.pad(c, ((0, 8 - BATCH), (0, 0)))
    mod = _ada(c_pad, w_ada, b_ada.reshape(DEPTH, 1, 6 * D_MODEL))
    mod = mod[:, :BATCH, :].reshape(DEPTH, BATCH, 6, D_MODEL)

    w_in_t = jnp.swapaxes(w_in, 1, 2)
    w_alow = jnp.pad(w_in_t[:, A_LOW_START:A_LOW_END, :],
                     ((0, 0), (0, RANK_PAD - GLA_GATE_RANK), (0, 0))).astype(BF16)
    wgu = jnp.pad(w_gate_up, ((0, 0), (0, RANK_PAD - GLA_GATE_RANK), (0, 0))).astype(BF16)
    bg = b_gate.reshape(DEPTH, 1, GLA_K)
    gn = gla_norm_gain.reshape(DEPTH, GLA_HEADS, 1, GLA_DV)
    plain_f32 = (w_gla_o, w_sb_o, w_out, w_ff1, w_ff2)
    w_main = _pack_w_in(0, w_in_t)
    w_gla_o_b, w_sb_o_b, w_out_b, w_ff1_b, w_ff2_b = (w[0].astype(BF16) for w in plain_f32)

    x2 = x.reshape(ROWS, D_MODEL)
    for l in range(DEPTH):
        p, a_low = _proj(l, x2, mod, norm_gains, w_main, w_alow)
        o_gla = _gla(l, p, a_low, wgu, bg, gn, tri)
        o_sb = _sb(p, tt)
        mixed = _mix(o_gla, o_sb, w_gla_o_b, w_sb_o_b, p)
        x2 = _out(l, mixed, w_out_b, x2, mod, norm_gains)
        if l + 1 < DEPTH:
            x2, w_main, w_gla_o_b, w_sb_o_b, w_out_b, w_ff1_b, w_ff2_b = _ffn(
                l, x2, mod, norm_gains, w_ff1_b, w_ff2_b, next_f32=(w_in_t,) + plain_f32)
        else:
            x2 = _ffn(l, x2, mod, norm_gains, w_ff1_b, w_ff2_b)
    return x2.reshape(BATCH, SEQ, D_MODEL)
```

```python
import numpy as np
import jax
import jax.numpy as jnp
from jax import lax
from jax.experimental import pallas as pl
from jax.experimental.pallas import tpu as pltpu

F32 = jnp.float32
BF16 = jnp.bfloat16

D_MODEL = 2048
BATCH = 4
SEQ = 2048
DEPTH = 4
ROWS = BATCH * SEQ

CHUNK = 64
GLA_HEADS = 4
GLA_DK = 256
GLA_DV = 512
GLA_K = GLA_HEADS * GLA_DK
GLA_V = GLA_HEADS * GLA_DV
GLA_GATE_RANK = 16
GLA_GATE_TAU = 16.0
SB_HEADS = 16
SB_DH = 128
SB_W = SB_HEADS * SB_DH
D_FF = 4 * D_MODEL
EPS = 1e-6

LOG2E = 1.4426950408889634
LANES = 128
RANK_PAD = LANES
A_LOW_START = 2 * GLA_K + 2 * GLA_V
A_LOW_END = A_LOW_START + GLA_GATE_RANK

P_COLS = 2 * GLA_K + 2 * GLA_V + 3 * SB_W + 2 * D_MODEL
COL_QA = 0
COL_KA = GLA_K
COL_VA = 2 * GLA_K
COL_RA = 2 * GLA_K + GLA_V
COL_QB = 2 * GLA_K + 2 * GLA_V
COL_KB = COL_QB + SB_W
COL_VB = COL_KB + SB_W
COL_GA = COL_VB + SB_W
COL_GB = COL_GA + D_MODEL

VMEM_LIMIT = 56 * 1024 * 1024


def _cparams(sem):
    return pltpu.CompilerParams(dimension_semantics=sem, vmem_limit_bytes=VMEM_LIMIT)


def _softplus(z):
    return jnp.maximum(z, 0.0) + jnp.log(1.0 + jnp.exp(-jnp.abs(z)))


def _split_bf16(a):
    hi = a.astype(BF16)
    lo = (a - hi.astype(F32)).astype(BF16)
    return hi, lo


ADA_TK = 256


def _ada_kernel(c_ref, w_ref, b_ref, o_ref):
    @pl.when(pl.program_id(1) == 0)
    def _():
        o_ref[...] = jnp.broadcast_to(b_ref[...], o_ref.shape)

    c = c_ref[...]
    ca = (c * jax.nn.sigmoid(c)).astype(BF16)
    o_ref[...] += jnp.dot(ca, w_ref[...].astype(BF16), preferred_element_type=F32)


def _ada(c_pad, w_ada, b_ada3):
    n = 6 * D_MODEL
    return pl.pallas_call(
        _ada_kernel,
        out_shape=jax.ShapeDtypeStruct((DEPTH, 8, n), F32),
        grid=(DEPTH, D_MODEL // ADA_TK),
        in_specs=[
            pl.BlockSpec((8, ADA_TK), lambda l, k: (0, k)),
            pl.BlockSpec((None, ADA_TK, n), lambda l, k: (l, k, 0)),
            pl.BlockSpec((None, 1, n), lambda l, k: (l, 0, 0)),
        ],
        out_specs=pl.BlockSpec((None, 8, n), lambda l, k: (l, 0, 0)),
        compiler_params=_cparams(("arbitrary", "arbitrary")),
        name="ada",
    )(c_pad, w_ada, b_ada3)


PROJ_TM = 1024
PROJ_TN = 2048
PACK_TN = 1024
PACK_LO_TILES = A_LOW_START // PACK_TN


ROW_CHUNK = 16
ROW_UNROLL = 8


def _for_row_chunks(n_rows, body):
    def step(i, carry):
        body(pl.ds(pl.multiple_of(i * ROW_CHUNK, ROW_CHUNK), ROW_CHUNK))
        return carry

    lax.fori_loop(0, n_rows // ROW_CHUNK, step, 0, unroll=ROW_UNROLL)


def _chunk_rows(*row_vectors):
    return [jnp.broadcast_to(v, (ROW_CHUNK, v.shape[-1])) for v in row_vectors]


def _modulated_norm(x, gain, shift, scale):
    ms = jnp.mean(x * x, axis=-1, keepdims=True)
    y = x * lax.rsqrt(ms + EPS) * gain
    return y * (1.0 + scale) + shift


def _gated_norm_residual(x, y, gain, gate):
    ms = jnp.mean(y * y, axis=-1, keepdims=True)
    return x + gate * (y * lax.rsqrt(ms + EPS) * gain)


def _packed_rows(tile, w, nxt, rows, lo_tiles):
    shifted = jnp.concatenate([w[GLA_GATE_RANK:rows, :], nxt], axis=0)
    return jnp.where(tile < lo_tiles, w, shifted).astype(BF16)


def _pack_specs(l, rows, tile_of):
    per = rows // GLA_GATE_RANK
    return [pl.BlockSpec((None, rows, D_MODEL), lambda *g: (l, tile_of(*g), 0)),
            pl.BlockSpec((None, GLA_GATE_RANK, D_MODEL),
                         lambda *g: (l, (tile_of(*g) + 1) * per, 0))]


def _pack_kernel(w_ref, nxt_ref, o_ref):
    o_ref[...] = _packed_rows(pl.program_id(0), w_ref[...], nxt_ref[...], PACK_TN, PACK_LO_TILES)


def _pack_w_in(l, w_in_t):
    tn = PACK_TN
    return pl.pallas_call(
        _pack_kernel,
        out_shape=jax.ShapeDtypeStruct((P_COLS, D_MODEL), BF16),
        grid=(P_COLS // tn,),
        in_specs=_pack_specs(l, tn, lambda j: j),
        out_specs=pl.BlockSpec((tn, D_MODEL), lambda j: (j, 0)),
        compiler_params=_cparams(("arbitrary",)),
        name="pack_w_in",
    )(w_in_t, w_in_t)


_NT = (((1,), (1,)), ((), ()))


def _proj_kernel(x_ref, mod_ref, ng_ref, wt_ref, wat_ref, p_ref, al_ref, h_ref):
    @pl.when(pl.program_id(1) == 0)
    def _():
        h = _modulated_norm(x_ref[...], ng_ref[0:1, :], mod_ref[0:1, :], mod_ref[1:2, :])
        hb = h.astype(BF16)
        h_ref[...] = hb
        al_ref[...] = lax.dot_general(hb, wat_ref[...], _NT, preferred_element_type=F32)

    p_ref[...] = lax.dot_general(h_ref[...], wt_ref[...], _NT,
                                 preferred_element_type=F32).astype(BF16)


def _mod_spec(l, tm):
    return pl.BlockSpec((None, None, 6, D_MODEL), lambda i, *_: (l, i * tm // SEQ, 0, 0))


def _gain_spec(l):
    return pl.BlockSpec((None, 4, D_MODEL), lambda *_: (l, 0, 0))


def _proj(l, x2, mod, norm_gains, w_main, w_alow):
    tm, tn = PROJ_TM, PROJ_TN
    return pl.pallas_call(
        _proj_kernel,
        out_shape=(jax.ShapeDtypeStruct((ROWS, P_COLS), BF16),
                   jax.ShapeDtypeStruct((ROWS, RANK_PAD), F32)),
        grid=(ROWS // tm, P_COLS // tn),
        in_specs=[
            pl.BlockSpec((tm, D_MODEL), lambda i, j: (i, 0)),
            _mod_spec(l, tm),
            _gain_spec(l),
            pl.BlockSpec((tn, D_MODEL), lambda i, j: (j, 0)),
            pl.BlockSpec((None, RANK_PAD, D_MODEL), lambda i, j: (l, 0, 0)),
        ],
        out_specs=(pl.BlockSpec((tm, tn), lambda i, j: (i, j)),
                   pl.BlockSpec((tm, RANK_PAD), lambda i, j: (i, 0))),
        scratch_shapes=[pltpu.VMEM((tm, D_MODEL), BF16)],
        compiler_params=_cparams(("arbitrary", "arbitrary")),
        name="proj",
    )(x2, mod, norm_gains, w_main, w_alow)


GLA_T = 512
GLA_HB = 4


def _gla_kernel(q_ref, k_ref, v_ref, r_ref, al_ref, wgu_ref, bg_ref, gn_ref, tri_ref,
                o_ref, st_ref):
    @pl.when(pl.program_id(2) == 0)
    def _():
        st_ref[...] = jnp.zeros_like(st_ref)

    a = jnp.dot(al_ref[...].astype(BF16), wgu_ref[...], preferred_element_type=F32) + bg_ref[...]
    log_a = -_softplus(-a) * (1.0 / GLA_GATE_TAU)
    tri = tri_ref[...]

    n_chunks = GLA_T // CHUNK

    def chunk_rows(j):
        return slice(j * CHUNK, (j + 1) * CHUNK)

    def chunk_kv(j, h):
        rows = chunk_rows(j)
        kcols = slice(h * GLA_DK, (h + 1) * GLA_DK)
        hi, lo = _split_bf16(log_a[rows, kcols])
        g = (jnp.dot(tri, hi, preferred_element_type=F32)
             + jnp.dot(tri, lo, preferred_element_type=F32))
        g_tot = g[CHUNK - 1:CHUNK, :]
        kd = (k_ref[rows, kcols].astype(F32) * jnp.exp(g_tot - g)).astype(BF16)
        kv_t = lax.dot_general(v_ref[rows, h * GLA_DV:(h + 1) * GLA_DV], kd,
                               (((0,), (0,)), ((), ())), preferred_element_type=F32)
        return jnp.exp(g_tot), kv_t

    ahead = 2
    pending = [[chunk_kv(j, h) for j in range(ahead)] for h in range(GLA_HB)]
    for j in range(n_chunks):
        rows = chunk_rows(j)
        for h in range(GLA_HB):
            vcols = slice(h * GLA_DV, (h + 1) * GLA_DV)
            decay, kv_t = pending[h].pop(0)
            if j + ahead < n_chunks:
                pending[h].append(chunk_kv(j + ahead, h))
            st = st_ref[h] * decay + kv_t
            st_ref[h] = st
            o = lax.dot_general(q_ref[rows, h * GLA_DK:(h + 1) * GLA_DK], st.astype(BF16), _NT,
                                preferred_element_type=F32) * (GLA_DK ** -0.5)
            ms = jnp.mean(o * o, axis=-1, keepdims=True)
            on = o * lax.rsqrt(ms + EPS) * gn_ref[h]
            r = r_ref[rows, vcols].astype(F32)
            o_ref[rows, vcols] = (on * (r * jax.nn.sigmoid(r))).astype(BF16)


def _gla(l, p, a_low, wgu, bg, gn, tri):
    t, hb = GLA_T, GLA_HB
    nt = SEQ // t
    wk, wv = hb * GLA_DK, hb * GLA_DV

    def row(b, h, i):
        return b * nt + i

    return pl.pallas_call(
        _gla_kernel,
        out_shape=jax.ShapeDtypeStruct((ROWS, GLA_V), BF16),
        grid=(BATCH, GLA_HEADS // hb, nt),
        in_specs=[
            pl.BlockSpec((t, wk), lambda b, h, i: (row(b, h, i), COL_QA // wk + h)),
            pl.BlockSpec((t, wk), lambda b, h, i: (row(b, h, i), COL_KA // wk + h)),
            pl.BlockSpec((t, wv), lambda b, h, i: (row(b, h, i), COL_VA // wv + h)),
            pl.BlockSpec((t, wv), lambda b, h, i: (row(b, h, i), COL_RA // wv + h)),
            pl.BlockSpec((t, RANK_PAD), lambda b, h, i: (row(b, h, i), 0)),
            pl.BlockSpec((None, RANK_PAD, wk), lambda b, h, i: (l, 0, h)),
            pl.BlockSpec((None, 1, wk), lambda b, h, i: (l, 0, h)),
            pl.BlockSpec((None, hb, 1, GLA_DV), lambda b, h, i: (l, h, 0, 0)),
            pl.BlockSpec((CHUNK, CHUNK), lambda b, h, i: (0, 0)),
        ],
        out_specs=pl.BlockSpec((t, wv), lambda b, h, i: (row(b, h, i), h)),
        scratch_shapes=[pltpu.VMEM((hb, GLA_DV, GLA_DK), F32)],
        compiler_params=_cparams(("arbitrary", "arbitrary", "arbitrary")),
        name="gla",
    )(p, p, p, p, a_low, wgu, bg, gn, tri)


SB_TQ = 256
SB_TK = 256
SB_HB = 8
SB_HW = SB_HB * SB_DH
SB_MASKED = -1e30


def _sb_kernel(q_ref, k_ref, v_ref, tt_ref, o_ref, acc_ref, carry_ref, z_ref, sp_ref, lb_ref):
    qi = pl.program_id(2)
    tt = tt_ref[...]
    scale = SB_DH ** -0.5 * LOG2E
    acc_ref[...] = jnp.zeros_like(acc_ref)
    carry_ref[...] = jnp.zeros_like(carry_ref)

    heads = [slice(h * SB_DH, (h + 1) * SB_DH) for h in range(SB_HB)]

    def keys(kb):
        return pl.ds(pl.multiple_of(kb * SB_TK, SB_TK), SB_TK)

    def step(w_kb=None, w_slot=None, s_slot=None, past=None, z_kb=None, z_slot=None):
        cs = [None] * SB_HB

        def matmuls(h):
            if w_kb is not None:
                cs[h] = jnp.dot(sp_ref[w_slot, h], tt, preferred_element_type=F32)
            if z_kb is not None:
                z_ref[z_slot, h] = lax.dot_general(q_ref[:, heads[h]], k_ref[keys(z_kb), heads[h]],
                                                   _NT, preferred_element_type=F32) * scale

        matmuls(0)
        for h in range(SB_HB):
            if h + 1 < SB_HB:
                matmuls(h + 1)
            if s_slot is not None:
                z = z_ref[s_slot, h]
                sp = jnp.maximum(z, 0.0) + jnp.log2(1.0 + jnp.exp2(-jnp.abs(z)))
                log_beta = z - sp
                if past is not None:
                    sp = jnp.where(past, sp, 0.0)
                    log_beta = jnp.where(past, log_beta, SB_MASKED)
                sp_ref[s_slot, h] = sp.astype(BF16)
                lb_ref[s_slot, h] = log_beta
            if w_kb is not None:
                carry = carry_ref[h]
                later = cs[h][:, :SB_TK] + jnp.concatenate([carry] * (SB_TK // LANES), axis=1)
                w = jnp.exp2(lb_ref[w_slot, h] - later).astype(BF16)
                carry_ref[h] = carry + cs[h][:, SB_TK:]
                acc_ref[h] += jnp.dot(w, v_ref[keys(w_kb), heads[h]],
                                      preferred_element_type=F32)

    def block(j):
        return jnp.maximum(qi - j, 0)

    t_idx = lax.broadcasted_iota(jnp.int32, (SB_TQ, SB_TK), 0)
    s_idx = lax.broadcasted_iota(jnp.int32, (SB_TQ, SB_TK), 1)
    step(z_kb=qi, z_slot=0)
    step(s_slot=0, past=s_idx < t_idx, z_kb=block(1), z_slot=1)

    def pair(i, _):
        j = 2 * i
        step(w_kb=qi - j, w_slot=0, s_slot=1, z_kb=block(j + 2), z_slot=0)
        step(w_kb=qi - j - 1, w_slot=1, s_slot=0, z_kb=block(j + 3), z_slot=1)
        return 0

    lax.fori_loop(0, qi // 2, pair, 0)

    @pl.when(qi % 2 == 1)
    def _():
        step(w_kb=1, w_slot=0, s_slot=1)
        step(w_kb=0, w_slot=1)

    @pl.when(qi % 2 == 0)
    def _():
        step(w_kb=0, w_slot=0)

    for h in range(SB_HB):
        o_ref[:, h * SB_DH:(h + 1) * SB_DH] = acc_ref[h].astype(BF16)


def _sb(p, tt):
    nq = SEQ // SB_TQ
    return pl.pallas_call(
        _sb_kernel,
        out_shape=jax.ShapeDtypeStruct((ROWS, SB_W), BF16),
        grid=(BATCH, SB_HEADS // SB_HB, nq),
        in_specs=[
            pl.BlockSpec((SB_TQ, SB_HW), lambda b, h, i: (b * nq + i, COL_QB // SB_HW + h)),
            pl.BlockSpec((SEQ, SB_HW), lambda b, h, i: (b, COL_KB // SB_HW + h)),
            pl.BlockSpec((SEQ, SB_HW), lambda b, h, i: (b, COL_VB // SB_HW + h)),
            pl.BlockSpec((SB_TK, SB_TK + LANES), lambda b, h, i: (0, 0)),
        ],
        out_specs=pl.BlockSpec((SB_TQ, SB_HW), lambda b, h, i: (b * nq + i, h)),
        scratch_shapes=[pltpu.VMEM((SB_HB, SB_TQ, SB_DH), F32),
                        pltpu.VMEM((SB_HB, SB_TQ, LANES), F32),
                        pltpu.VMEM((2, SB_HB, SB_TQ, SB_TK), F32),
                        pltpu.VMEM((2, SB_HB, SB_TQ, SB_TK), BF16),
                        pltpu.VMEM((2, SB_HB, SB_TQ, SB_TK), F32)],
        compiler_params=_cparams(("arbitrary", "arbitrary", "arbitrary")),
        name="sb",
    )(p, p, p, tt)


MIX_TM = 1024
MIX_TN = 1024


def _mix_kernel(oa_ref, ob_ref, wa_ref, wb_ref, ga_ref, gb_ref, o_ref):
    ga = jax.nn.sigmoid(ga_ref[...].astype(F32))
    gb = jax.nn.sigmoid(gb_ref[...].astype(F32))
    ya = jnp.dot(oa_ref[...], wa_ref[...], preferred_element_type=F32)
    yb = jnp.dot(ob_ref[...], wb_ref[...], preferred_element_type=F32)
    o_ref[...] = (ga * ya + gb * yb).astype(BF16)


def _mix(o_gla, o_sb, w_gla_o, w_sb_o, p):
    tm, tn = MIX_TM, MIX_TN
    return pl.pallas_call(
        _mix_kernel,
        out_shape=jax.ShapeDtypeStruct((ROWS, D_MODEL), BF16),
        grid=(ROWS // tm, D_MODEL // tn),
        in_specs=[
            pl.BlockSpec((tm, GLA_V), lambda i, j: (i, 0)),
            pl.BlockSpec((tm, SB_W), lambda i, j: (i, 0)),
            pl.BlockSpec((GLA_V, tn), lambda i, j: (0, j)),
            pl.BlockSpec((SB_W, tn), lambda i, j: (0, j)),
            pl.BlockSpec((tm, tn), lambda i, j: (i, COL_GA // tn + j)),
            pl.BlockSpec((tm, tn), lambda i, j: (i, COL_GB // tn + j)),
        ],
        out_specs=pl.BlockSpec((tm, tn), lambda i, j: (i, j)),
        compiler_params=_cparams(("arbitrary", "arbitrary")),
        name="mix",
    )(o_gla, o_sb, w_gla_o, w_sb_o, p, p)


OUT_TM = 512


def _out_kernel(m_ref, w_ref, x_ref, mod_ref, ng_ref, o_ref):
    y = jnp.dot(m_ref[...], w_ref[...], preferred_element_type=F32)
    o_ref[...] = _gated_norm_residual(x_ref[...], y, ng_ref[1:2, :], mod_ref[2:3, :])


def _out(l, mixed, w_out, x2, mod, norm_gains):
    tm = OUT_TM
    return pl.pallas_call(
        _out_kernel,
        out_shape=jax.ShapeDtypeStruct((ROWS, D_MODEL), F32),
        grid=(ROWS // tm,),
        in_specs=[
            pl.BlockSpec((tm, D_MODEL), lambda i: (i, 0)),
            pl.BlockSpec((D_MODEL, D_MODEL), lambda i: (0, 0)),
            pl.BlockSpec((tm, D_MODEL), lambda i: (i, 0)),
            _mod_spec(l, tm),
            _gain_spec(l),
        ],
        out_specs=pl.BlockSpec((tm, D_MODEL), lambda i: (i, 0)),
        compiler_params=_cparams(("arbitrary",)),
        name="out",
    )(mixed, w_out, x2, mod, norm_gains)


FFN_TM = 512
FFN_TF = 1024
FFN_STEPS = (ROWS // FFN_TM) * (D_FF // FFN_TF)
CAST_IN_ROWS = P_COLS // FFN_STEPS
CAST_IN_LO_TILES = A_LOW_START // CAST_IN_ROWS
N_PLAIN_CASTS = 5


def _ffn_cast_kernel(x_ref, mod_ref, ng_ref, w1_ref, w2_ref, win_ref, winx_ref, *rest):
    plain_in = rest[:N_PLAIN_CASTS]
    o_ref, win_o = rest[N_PLAIN_CASTS:N_PLAIN_CASTS + 2]
    plain_out = rest[N_PLAIN_CASTS + 2:2 * N_PLAIN_CASTS + 2]
    h_ref, acc_ref = rest[2 * N_PLAIN_CASTS + 2:]

    def cast_slabs():
        step = pl.program_id(0) * pl.num_programs(1) + pl.program_id(1)
        win_o[...] = _packed_rows(step, win_ref[...], winx_ref[...], CAST_IN_ROWS, CAST_IN_LO_TILES)
        for src, dst in zip(plain_in, plain_out):
            dst[...] = src[...].astype(BF16)

    _ffn_kernel(x_ref, mod_ref, ng_ref, w1_ref, w2_ref, o_ref, h_ref, acc_ref, beside_dots=cast_slabs)


def _ffn_kernel(x_ref, mod_ref, ng_ref, w1_ref, w2_ref, o_ref, h_ref, acc_ref, beside_dots=None):
    f = pl.program_id(1)

    @pl.when(f == 0)
    def _():
        gain, shift, scale = _chunk_rows(ng_ref[2:3, :], mod_ref[3:4, :], mod_ref[4:5, :])

        def norm_rows(rows):
            h_ref[rows, :] = _modulated_norm(x_ref[rows, :], gain, shift, scale).astype(BF16)

        _for_row_chunks(FFN_TM, norm_rows)
        acc_ref[...] = jnp.zeros_like(acc_ref)

    if beside_dots is not None:
        beside_dots()
    a = jnp.dot(h_ref[...], w1_ref[...], preferred_element_type=F32)
    a = jnp.square(jnp.maximum(a, 0.0)).astype(BF16)
    acc_ref[...] += jnp.dot(a, w2_ref[...], preferred_element_type=F32)

    @pl.when(f == pl.num_programs(1) - 1)
    def _():
        gain, gate = _chunk_rows(ng_ref[3:4, :], mod_ref[5:6, :])

        def finish_rows(rows):
            o_ref[rows, :] = _gated_norm_residual(x_ref[rows, :], acc_ref[rows, :], gain, gate)

        _for_row_chunks(FFN_TM, finish_rows)


def _ffn(l, x2, mod, norm_gains, w1, w2, next_f32=None):
    tm, tf = FFN_TM, FFN_TF
    nf = D_FF // tf
    in_specs = [
        pl.BlockSpec((tm, D_MODEL), lambda i, f: (i, 0)),
        _mod_spec(l, tm),
        _gain_spec(l),
        pl.BlockSpec((D_MODEL, tf), lambda i, f: (0, f)),
        pl.BlockSpec((tf, D_MODEL), lambda i, f: (f, 0)),
    ]
    x_spec = pl.BlockSpec((tm, D_MODEL), lambda i, f: (i, 0))
    x_shape = jax.ShapeDtypeStruct((ROWS, D_MODEL), F32)
    scratch = [pltpu.VMEM((tm, D_MODEL), BF16), pltpu.VMEM((tm, D_MODEL), F32)]
    params = _cparams(("arbitrary", "arbitrary"))
    if next_f32 is None:
        return pl.pallas_call(
            _ffn_kernel, out_shape=x_shape, grid=(ROWS // tm, nf), in_specs=in_specs,
            out_specs=x_spec, scratch_shapes=scratch, compiler_params=params, name="ffn",
        )(x2, mod, norm_gains, w1, w2)

    def step(i, f):
        return i * nf + f

    def slab(rows, cols):
        return pl.BlockSpec((rows, cols), lambda i, f: (step(i, f), 0))

    def slab_of_next(rows, cols):
        return pl.BlockSpec((None, rows, cols), lambda i, f: (l + 1, step(i, f), 0))

    w_in_t, plain = next_f32[0], next_f32[1:]
    assert len(plain) == N_PLAIN_CASTS
    shapes = [w.shape[1:] for w in plain]
    slabs = [(r // FFN_STEPS, c) for r, c in shapes]
    return pl.pallas_call(
        _ffn_cast_kernel,
        out_shape=(x_shape, jax.ShapeDtypeStruct((P_COLS, D_MODEL), BF16),
                   *[jax.ShapeDtypeStruct(s, BF16) for s in shapes]),
        grid=(ROWS // tm, nf),
        in_specs=(in_specs + _pack_specs(l + 1, CAST_IN_ROWS, step)
                  + [slab_of_next(r, c) for r, c in slabs]),
        out_specs=(x_spec, slab(CAST_IN_ROWS, D_MODEL), *[slab(r, c) for r, c in slabs]),
        scratch_shapes=scratch, compiler_params=params, name="ffn",
    )(x2, mod, norm_gains, w1, w2, w_in_t, w_in_t, *plain)


def _cumsum_constants():
    j = np.arange(CHUNK)
    tri = (j[:, None] >= j[None, :]).astype(np.float32)
    jj = np.arange(SB_TK)
    s = np.arange(SB_TK + LANES)
    tt =((jj[:, None] > s[None, :]) | (s[None, :] >= SB_TK)).astype(np.float32)
    return jnp.asarray(tri, BF16), jnp.asarray(tt, BF16)


def kernel(x, c, w_ada, b_ada, norm_gains, w_in, w_gate_up, b_gate, gla_norm_gain,
           w_gla_o, w_sb_o, w_out, w_ff1, w_ff2):
    tri, tt = _cumsum_constants()

    c_pad = jnp.pad(c, ((0, 8 - BATCH), (0, 0)))
    mod = _ada(c_pad, w_ada, b_ada.reshape(DEPTH, 1, 6 * D_MODEL))
    mod = mod[:, :BATCH, :].reshape(DEPTH, BATCH, 6, D_MODEL)

    w_in_t = jnp.swapaxes(w_in, 1, 2)
    w_alow = jnp.pad(w_in_t[:, A_LOW_START:A_LOW_END, :],
                     ((0, 0), (0, RANK_PAD - GLA_GATE_RANK), (0, 0))).astype(BF16)
    wgu = jnp.pad(w_gate_up, ((0, 0), (0, RANK_PAD - GLA_GATE_RANK), (0, 0))).astype(BF16)
    bg = b_gate.reshape(DEPTH, 1, GLA_K)
    gn = gla_norm_gain.reshape(DEPTH, GLA_HEADS, 1, GLA_DV)
    plain_f32 = (w_gla_o, w_sb_o, w_out, w_ff1, w_ff2)
    w_main = _pack_w_in(0, w_in_t)
    w_gla_o_b, w_sb_o_b, w_out_b, w_ff1_b, w_ff2_b = (w[0].astype(BF16) for w in plain_f32)

    x2 = x.reshape(ROWS, D_MODEL)
    for l in range(DEPTH):
        p, a_low = _proj(l, x2, mod, norm_gains, w_main, w_alow)
        o_gla = _gla(l, p, a_low, wgu, bg, gn, tri)
        o_sb = _sb(p, tt)
        mixed = _mix(o_gla, o_sb, w_gla_o_b, w_sb_o_b, p)
        x2 = _out(l, mixed, w_out_b, x2, mod, norm_gains)
        if l + 1 < DEPTH:
            x2, w_main, w_gla_o_b, w_sb_o_b, w_out_b, w_ff1_b, w_ff2_b = _ffn(
                l, x2, mod, norm_gains, w_ff1_b, w_ff2_b, next_f32=(w_in_t,) + plain_f32)
        else:
            x2 = _ffn(l, x2, mod, norm_gains, w_ff1_b, w_ff2_b)
    return x2.reshape(BATCH, SEQ, D_MODEL)
```
